```python
import jax, jax.numpy as jnp
from jax import lax
import numpy as np

D_MODEL = 1024
BATCH = 2
SEQ = 8192
DEPTH = 4
DEC_BATCH = 32
DEC_SEQ = 8
PAST_LEN = 8192
PAGE_SIZE = 128

H_A = 8
DH_A = 64
MOBA_BLOCK = 256
MOBA_TOPK = 3
Q_BLOCK = 64
ROPE_THETA = 500000.0
ROT_DIM = DH_A // 4
H_R = 4
DK_R = 128
DV_R = 256
RET_CHUNK = 128
RET_THETA = 10000.0
D_FF = -(-8 * D_MODEL // (3 * 256)) * 256
W_A = H_A * DH_A
W_RQK = H_R * DK_R
W_RV = H_R * DV_R
D_IN = 3 * W_A + 2 * W_RQK + 2 * W_RV + 2 * D_MODEL
EPS = 1e-6
NEG = -1e30

kernel_name = "moba_retention_gated_hybrid_step"


def rmsnorm(x, g):
    xf = x.astype(jnp.float32)
    y = xf * lax.rsqrt(jnp.mean(xf * xf, axis=-1, keepdims=True) + EPS) * g.astype(jnp.float32)
    return y.astype(x.dtype)


def attn_freqs():
    return ROPE_THETA ** (-jnp.arange(0, ROT_DIM, 2, dtype=jnp.float32) / ROT_DIM)


def ret_freqs():
    return 1.0 / (RET_THETA ** jnp.linspace(0.0, 1.0, DK_R // 2, dtype=jnp.float32))


def rope(x, pos, freqs):
    rd = 2 * freqs.shape[0]
    ang = pos.astype(jnp.float32)[:, None] * freqs[None, :]
    cos = jnp.cos(ang)[None, :, None, :]
    sin = jnp.sin(ang)[None, :, None, :]
    xf = x.astype(jnp.float32)
    x1 = xf[..., : rd // 2]
    x2 = xf[..., rd // 2: rd]
    out = jnp.concatenate([x1 * cos - x2 * sin, x2 * cos + x1 * sin, xf[..., rd:]], axis=-1)
    return out.astype(x.dtype)


def _moba_query_block(q, pos, kb, vb, kmean, k_eff):
    B, Qc, H, dh = q.shape
    nb = kb.shape[2]
    qf = q.astype(jnp.float32) * (DH_A ** -0.5)
    bt = pos // MOBA_BLOCK
    s = jnp.einsum('bqhd,bhnd->bhqn', qf, kmean)
    past = jnp.arange(nb)[None, :] < bt[:, None]
    s = jnp.where(past[None, None], s, -jnp.inf)
    _, sel = lax.top_k(s, k_eff)
    own = jnp.broadcast_to(bt[None, None, :, None], (B, H, Qc, 1)).astype(sel.dtype)
    idx = jnp.concatenate([sel, own], axis=-1)
    gather = jax.vmap(jax.vmap(lambda t, i: t[i]))
    kg = gather(kb, idx).astype(jnp.float32)
    vg = gather(vb, idx).astype(jnp.float32)
    logits = jnp.einsum('bqhd,bhqjsd->bhqjs', qf, kg)
    key_pos = idx[..., None] * MOBA_BLOCK + jnp.arange(MOBA_BLOCK)
    is_own = jnp.arange(k_eff + 1) == k_eff
    blk_ok = is_own | (idx < bt[None, None, :, None])
    valid = blk_ok[..., None] & (key_pos <= pos[None, None, :, None, None])
    logits = jnp.where(valid, logits, NEG)
    p = jax.nn.softmax(logits.reshape(B, H, Qc, -1), axis=-1).reshape(logits.shape)
    out = jnp.einsum('bhqjs,bhqjsd->bqhd', p, vg)
    return out.astype(q.dtype)


def moba_attend(q, pos, k_all, v_all):
    B, Tq, H, dh = q.shape
    Tk = k_all.shape[1]
    nb = -(-Tk // MOBA_BLOCK)
    pad = nb * MOBA_BLOCK - Tk
    kb = jnp.pad(k_all, ((0, 0), (0, pad), (0, 0), (0, 0))).reshape(B, nb, MOBA_BLOCK, H, dh).transpose(0, 3, 1, 2, 4)
    vb = jnp.pad(v_all, ((0, 0), (0, pad), (0, 0), (0, 0))).reshape(B, nb, MOBA_BLOCK, H, dh).transpose(0, 3, 1, 2, 4)
    kmean = jnp.mean(kb.astype(jnp.float32), axis=3)
    k_eff = min(MOBA_TOPK, nb)
    fn = lambda args: _moba_query_block(args[0], args[1], kb, vb, kmean, k_eff)
    if Tq > Q_BLOCK and Tq % Q_BLOCK == 0:
        n = Tq // Q_BLOCK
        qs = q.reshape(B, n, Q_BLOCK, H, dh).transpose(1, 0, 2, 3, 4)
        ps = pos.reshape(n, Q_BLOCK)
        out = lax.map(fn, (qs, ps))
        return out.transpose(1, 0, 2, 3, 4).reshape(B, Tq, H, dh)
    return fn((q, pos))


def retention(q, k, v, S0):
    B, T, H, dk = q.shape
    dv = v.shape[-1]
    C = RET_CHUNK if T % RET_CHUNK == 0 else T
    n = T // C
    log_g = jnp.log(1.0 - 2.0 ** (-5.0 - jnp.arange(H, dtype=jnp.float32)))
    i = jnp.arange(C, dtype=jnp.float32)
    diff = i[:, None] - i[None, :]
    causal = diff >= 0
    D = jnp.where(causal[None], jnp.exp(jnp.where(causal[None], diff[None] * log_g[:, None, None], 0.0)), 0.0)
    q_dec = jnp.exp((i[:, None] + 1.0) * log_g[None, :])
    k_dec = jnp.exp((C - 1.0 - i[:, None]) * log_g[None, :])
    c_dec = jnp.exp(C * log_g)
    qf = q.astype(jnp.float32)
    kf = k.astype(jnp.float32) * (dk ** -0.5)
    vf = v.astype(jnp.float32)
    to_chunks = lambda t: jnp.moveaxis(t.reshape(B, n, C, H, t.shape[-1]), 1, 0)

    def step(S, xs):
        qc, kc, vc = xs
        att = jnp.einsum('bihd,bjhd->bhij', qc, kc) * D[None]
        inner = jnp.einsum('bhij,bjhv->bihv', att, vc)
        cross = jnp.einsum('bihd,bhdv->bihv', qc, S) * q_dec[None, :, :, None]
        S_new = S * c_dec[None, :, None, None] + jnp.einsum('bjhd,bjhv->bhdv', kc * k_dec[None, :, :, None], vc)
        return S_new, inner + cross

    S_fin, o = lax.scan(step, S0.astype(jnp.float32), (to_chunks(qf), to_chunks(kf), to_chunks(vf)))
    o = jnp.moveaxis(o, 0, 1).reshape(B, T, H, dv)
    return o, S_fin.astype(S0.dtype)


def layer(x, pos, g_mix_pre, w_in, w_proj_attn, w_proj_ret, w_out, g_mix_post,
          g_ffn_pre, w_ffn_in, w_ffn_out, g_ffn_post, k_past, v_past, S0):
    B, T, _ = x.shape
    h = rmsnorm(x, g_mix_pre)
    z = h @ w_in
    pts = list(np.cumsum([W_A, W_A, W_A, W_RQK, W_RQK, W_RV, W_RV, D_MODEL]))
    qa, ka, va, qr, kr, vr, gr, gate_a, gate_r = jnp.split(z, pts, axis=-1)
    qa = rope(qa.reshape(B, T, H_A, DH_A), pos, attn_freqs())
    ka = rope(ka.reshape(B, T, H_A, DH_A), pos, attn_freqs())
    va = va.reshape(B, T, H_A, DH_A)
    if k_past is None:
        k_all, v_all = ka, va
    else:
        k_all = jnp.concatenate([k_past, ka.astype(k_past.dtype)], axis=1)
        v_all = jnp.concatenate([v_past, va.astype(v_past.dtype)], axis=1)
    a = moba_attend(qa, pos, k_all, v_all).reshape(B, T, W_A)
    qr = rope(qr.reshape(B, T, H_R, DK_R), pos, ret_freqs())
    kr = rope(kr.reshape(B, T, H_R, DK_R), pos, ret_freqs())
    o_r, S_new = retention(qr, kr, vr.reshape(B, T, H_R, DV_R), S0)
    o_r = o_r * lax.rsqrt(jnp.mean(o_r * o_r, axis=-1, keepdims=True) + EPS)
    r = jax.nn.silu(gr) * o_r.reshape(B, T, W_RV).astype(x.dtype)
    merged = jax.nn.sigmoid(gate_a) * (a @ w_proj_attn) + jax.nn.sigmoid(gate_r) * (r @ w_proj_ret)
    x = x + rmsnorm(merged @ w_out, g_mix_post)
    h = rmsnorm(x, g_ffn_pre)
    gt, up = jnp.split(h @ w_ffn_in, 2, axis=-1)
    x = x + rmsnorm((jax.nn.silu(gt) * up) @ w_ffn_out, g_ffn_post)
    return x, ka, va, S_new


def setup_inputs(seed: int = 0) -> dict:
    key = jax.random.key(seed)
    ks = jax.random.split(key, 20)
    n_pages = PAST_LEN // PAGE_SIZE
    n_used = DEC_BATCH * n_pages
    n_pool = n_used + max(1, n_used // 4)
    f32 = jnp.float32
    nrm = lambda k, shape, s: jax.random.normal(k, shape, f32) * s
    gain = lambda k: 1.0 + 0.05 * jax.random.normal(k, (DEPTH, D_MODEL), f32)
    page_table = jax.random.permutation(ks[5], n_pool)[:n_used].reshape(DEC_BATCH, n_pages).astype(jnp.int32)
    return {
        "x_prompt": nrm(ks[0], (BATCH, SEQ, D_MODEL), 1.0),
        "x_sample": nrm(ks[1], (DEC_BATCH, DEC_SEQ, D_MODEL), 1.0),
        "cache_k": nrm(ks[2], (DEPTH, n_pool, PAGE_SIZE, H_A, DH_A), 1.0),
        "cache_v": nrm(ks[3], (DEPTH, n_pool, PAGE_SIZE, H_A, DH_A), 1.0),
        "state_ret": nrm(ks[4], (DEPTH, DEC_BATCH, H_R, DK_R, DV_R), 0.1),
        "page_table": page_table,
        "g_mix_pre": gain(ks[6]),
        "w_in": nrm(ks[7], (DEPTH, D_MODEL, D_IN), D_MODEL ** -0.5),
        "w_proj_attn": nrm(ks[8], (DEPTH, W_A, D_MODEL), W_A ** -0.5),
        "w_proj_ret": nrm(ks[9], (DEPTH, W_RV, D_MODEL), W_RV ** -0.5),
        "w_out": nrm(ks[10], (DEPTH, D_MODEL, D_MODEL), D_MODEL ** -0.5),
        "g_mix_post": gain(ks[11]),
        "g_ffn_pre": gain(ks[12]),
        "w_ffn_in": nrm(ks[13], (DEPTH, D_MODEL, 2 * D_FF), D_MODEL ** -0.5),
        "w_ffn_out": nrm(ks[14], (DEPTH, D_FF, D_MODEL), D_FF ** -0.5),
        "g_ffn_post": gain(ks[15]),
    }


def reference(x_prompt, x_sample, cache_k, cache_v, state_ret, page_table,
              g_mix_pre, w_in, w_proj_attn, w_proj_ret, w_out, g_mix_post,
              g_ffn_pre, w_ffn_in, w_ffn_out, g_ffn_post):
    B, T, _ = x_prompt.shape
    DB, TS, _ = x_sample.shape
    n_pages = page_table.shape[1]
    past_len = n_pages * cache_k.shape[2]
    pos_p = jnp.arange(T, dtype=jnp.int32)
    pos_s = past_len + jnp.arange(TS, dtype=jnp.int32)
    S0_p = jnp.zeros((B, H_R, DK_R, DV_R), jnp.float32)
    xp, xs = x_prompt, x_sample
    kp, vp, sp, kss, vss, sss = [], [], [], [], [], []
    for l in range(DEPTH):
        w = (g_mix_pre[l], w_in[l], w_proj_attn[l], w_proj_ret[l], w_out[l], g_mix_post[l],
             g_ffn_pre[l], w_ffn_in[l], w_ffn_out[l], g_ffn_post[l])
        xp, k_l, v_l, s_l = layer(xp, pos_p, *w, None, None, S0_p)
        kp.append(k_l); vp.append(v_l); sp.append(s_l)
        k_past = cache_k[l][page_table].reshape(DB, past_len, H_A, DH_A)
        v_past = cache_v[l][page_table].reshape(DB, past_len, H_A, DH_A)
        xs, k_l, v_l, s_l = layer(xs, pos_s, *w, k_past, v_past, state_ret[l])
        kss.append(k_l); vss.append(v_l); sss.append(s_l)
    return (xp, xs, jnp.stack(kp), jnp.stack(vp), jnp.stack(sp), jnp.stack(kss), jnp.stack(vss), jnp.stack(sss))
```

```python
import functools

import jax
import jax.numpy as jnp
from jax import lax
from jax.experimental import pallas as pl
from jax.experimental.pallas import tpu as pltpu

F32 = jnp.float32
BF16 = jnp.bfloat16

H_A = 8
DH_A = 64
MOBA_BLOCK = 256
MOBA_TOPK = 3
ROPE_THETA = 500000.0
ROT_DIM = DH_A // 4
H_R = 4
DK_R = 128
DV_R = 256
RET_CHUNK = 128
RET_THETA = 10000.0
EPS = 1e-6
NEG = -1e30
W_A = H_A * DH_A
W_RQK = H_R * DK_R
W_RV = H_R * DV_R

LANES = 128
VMEM_LIMIT = 56 * 1024 * 1024
HIGHEST = lax.Precision.HIGHEST


def _cparams(n_grid):
    return pltpu.CompilerParams(dimension_semantics=("arbitrary",) * n_grid,
                                vmem_limit_bytes=VMEM_LIMIT)


def _resident(shape):
    return pl.BlockSpec(shape, lambda *_: (0,) * len(shape), pipeline_mode=pl.Buffered(1))


def _rms_scale(y, g):
    return y * lax.rsqrt(jnp.mean(y * y, axis=-1, keepdims=True) + EPS) * g


def _dot(a, b):
    return jnp.dot(a, b, preferred_element_type=F32)


def _dot_nt(a, b, precision=None):
    return lax.dot_general(a, b, (((1,), (1,)), ((), ())), precision=precision,
                           preferred_element_type=F32)


def _in_proj_kernel(x_ref, g_ref, w_ref, ca_ref, s1_ref, s2_ref, cr_ref, sr_ref,
                    qa_ref, ka_ref, va_ref, kb_ref, vb_ref, qr_ref, kr_ref, vr_ref,
                    gr_ref, sga_ref, sgr_ref):
    d_model = x_ref.shape[1]
    h = _rms_scale(x_ref[...], g_ref[...]).astype(BF16)

    def proj(c0, n):
        return _dot(h, w_ref[:, c0:c0 + n])

    ca, s1, s2 = ca_ref[...], s1_ref[...], s2_ref[...]
    cr, sr = cr_ref[...], sr_ref[...]

    def rope_a(z):
        outs = []
        for c in range(z.shape[1] // LANES):
            zc = z[:, c * LANES:(c + 1) * LANES]
            outs.append(zc * ca + pltpu.roll(zc, ROT_DIM // 2, 1) * s1
                        + pltpu.roll(zc, LANES - ROT_DIM // 2, 1) * s2)
        return jnp.concatenate(outs, axis=1)

    def rope_r(z):
        outs = []
        for c in range(z.shape[1] // LANES):
            zc = z[:, c * LANES:(c + 1) * LANES]
            outs.append(zc * cr + pltpu.roll(zc, DK_R // 2, 1) * sr)
        return jnp.concatenate(outs, axis=1)

    c0 = 0
    qa_ref[...] = (rope_a(proj(c0, W_A)) * (DH_A ** -0.5)).astype(qa_ref.dtype)
    c0 += W_A
    ka = rope_a(proj(c0, W_A))
    ka_ref[...] = ka
    kb_ref[...] = ka.astype(BF16)
    c0 += W_A
    va = proj(c0, W_A)
    va_ref[...] = va
    vb_ref[...] = va.astype(BF16)
    c0 += W_A
    qr_ref[...] = rope_r(proj(c0, W_RQK)).astype(qr_ref.dtype)
    c0 += W_RQK
    kr_ref[...] = (rope_r(proj(c0, W_RQK)) * (DK_R ** -0.5)).astype(kr_ref.dtype)
    c0 += W_RQK
    vr_ref[...] = proj(c0, W_RV).astype(vr_ref.dtype)
    c0 += W_RV
    gr = proj(c0, W_RV)
    gr_ref[...] = (gr * jax.nn.sigmoid(gr)).astype(gr_ref.dtype)
    c0 += W_RV
    sga_ref[...] = jax.nn.sigmoid(proj(c0, d_model)).astype(sga_ref.dtype)
    c0 += d_model
    sgr_ref[...] = jax.nn.sigmoid(proj(c0, d_model)).astype(sgr_ref.dtype)


def _in_proj(x, g, w, tabs, tm, n_tab_blocks):
    n, d = x.shape
    d_in = w.shape[1]
    rows = lambda width: pl.BlockSpec((tm, width), lambda i: (i, 0))
    tab = pl.BlockSpec((tm, LANES), lambda i: (i % n_tab_blocks, 0))
    widths = [W_A, W_A, W_A, W_A, W_A, W_RQK, W_RQK, W_RV, W_RV, d, d]
    dtypes = [BF16, F32, F32, BF16, BF16, BF16, BF16, BF16, BF16, BF16, BF16]
    return pl.pallas_call(
        _in_proj_kernel,
        grid=(n // tm,),
        in_specs=[rows(d), _resident((1, d)), _resident((d, d_in))] + [tab] * 5,
        out_specs=[rows(wd) for wd in widths],
        out_shape=[jax.ShapeDtypeStruct((n, wd), dt) for wd, dt in zip(widths, dtypes)],
        compiler_params=_cparams(1),
        name="in_proj",
    )(x, g, w, *tabs)


def _top3_rows(s, valid_rows):
    nb = s.shape[0]
    blk = lax.broadcasted_iota(jnp.int32, s.shape, 0).astype(F32)
    sel = jnp.zeros(s.shape, F32)
    for _ in range(MOBA_TOPK):
        m = jnp.max(s, axis=0, keepdims=True)
        idx = jnp.min(jnp.where(s == m, blk, float(nb)), axis=0, keepdims=True)
        pick = blk == idx
        sel = jnp.where(pick, 1.0, sel)
        s = jnp.where(pick, -jnp.inf, s)
    if valid_rows is not None:
        sel = jnp.where(blk < valid_rows.astype(F32), sel, 0.0)
    return sel


def _moba_prompt_kernel(q_ref, k_ref, v_ref, o_ref, kmean_ref, vt_ref, sel_ref):
    nb = k_ref.shape[0]
    i = pl.program_id(2)

    @pl.when(i == 0)
    def _():
        for j in range(nb):
            kmean_ref[j:j + 1, :] = jnp.mean(k_ref[j].astype(F32), axis=0, keepdims=True)
            vt_ref[j] = v_ref[j].astype(F32).T.astype(BF16)

    qt = q_ref[...].astype(F32).T
    drow = lax.broadcasted_iota(jnp.int32, qt.shape, 0)
    qts = [jnp.where(drow < DH_A, qt, 0.0), jnp.where(drow >= DH_A, qt, 0.0)]
    qtb = [t.astype(BF16) for t in qts]

    km = kmean_ref[...]
    blk = lax.broadcasted_iota(jnp.int32, (nb, qt.shape[1]), 0)
    for hh in range(2):
        s = jnp.dot(km, qts[hh], precision=HIGHEST, preferred_element_type=F32)
        s = jnp.where(blk < i, s, -jnp.inf)
        sel = _top3_rows(s, i)
        for j in range(nb):
            sel_ref[hh, j] = jnp.broadcast_to(sel[j:j + 1, :], (8, sel.shape[1]))

    def block_update(kblk, vtblk, masks, carry):
        new = []
        for hh in range(2):
            m, l, acc = carry[hh]
            lg = jnp.where(masks[hh], _dot(kblk, qtb[hh]), NEG)
            m_new = jnp.maximum(m, jnp.max(lg, axis=0, keepdims=True))
            p = jnp.exp(lg - m_new)
            alpha = jnp.exp(m - m_new)
            l = alpha * l + jnp.sum(p, axis=0, keepdims=True)
            acc = alpha * acc + _dot(vtblk[hh * DH_A:(hh + 1) * DH_A, :], p.astype(BF16))
            new.append((m_new, l, acc))
        return tuple(new)

    nq = qt.shape[1]
    kpos = lax.broadcasted_iota(jnp.int32, (MOBA_BLOCK, nq), 0)
    qpos = lax.broadcasted_iota(jnp.int32, (MOBA_BLOCK, nq), 1)
    causal = kpos <= qpos
    init = tuple((jnp.full((1, nq), NEG, F32), jnp.zeros((1, nq), F32), jnp.zeros((DH_A, nq), F32))
                 for _ in range(2))
    carry = block_update(k_ref[i], vt_ref[i], (causal, causal), init)

    def body(j, carry):
        masks = tuple(sel_ref[hh, j][0:1, :] > 0.0 for hh in range(2))
        return block_update(k_ref[j], vt_ref[j], masks, carry)

    carry = lax.fori_loop(0, i, body, carry)
    out_t = jnp.concatenate([carry[hh][2] / carry[hh][1] for hh in range(2)], axis=0)
    o_ref[...] = out_t.T.astype(o_ref.dtype)


def _moba_prompt(qa, kb, vb, batch, t):
    nb = t // MOBA_BLOCK
    npair = W_A // LANES
    q3 = qa.reshape(batch, t, W_A)
    k4 = kb.reshape(batch, nb, MOBA_BLOCK, W_A)
    v4 = vb.reshape(batch, nb, MOBA_BLOCK, W_A)
    kv_spec = pl.BlockSpec((None, nb, MOBA_BLOCK, LANES), lambda b, hp, i: (b, 0, 0, hp))
    q_spec = pl.BlockSpec((None, MOBA_BLOCK, LANES), lambda b, hp, i: (b, i, hp))
    out = pl.pallas_call(
        _moba_prompt_kernel,
        grid=(batch, npair, nb),
        in_specs=[q_spec, kv_spec, kv_spec],
        out_specs=q_spec,
        out_shape=jax.ShapeDtypeStruct((batch, t, W_A), BF16),
        scratch_shapes=[pltpu.VMEM((nb, LANES), F32),
                        pltpu.VMEM((nb, LANES, MOBA_BLOCK), BF16),
                        pltpu.VMEM((2, nb, 8, MOBA_BLOCK), F32)],
        compiler_params=_cparams(3),
        name="moba_prompt",
    )(q3, k4, v4)
    return out.reshape(batch * t, W_A)


def _lane_head(lane):
    return lax.shift_right_logical(lane, DH_A.bit_length() - 1)


def _moba_sample_kernel(pt_ref, q_ref, kn_ref, vn_ref, *rest, pps, nchunk, page):
    del pt_ref
    k_pages = rest[:pps]
    v_pages = rest[pps:2 * pps]
    o_ref = rest[2 * pps]
    lg_ref, km_ref, sel_ref, inv_ref, acc_ref, qrows_ref = rest[2 * pps + 1:]
    ts = q_ref.shape[0]
    ncol = H_A * ts
    nb = km_ref.shape[0]
    ppb = MOBA_BLOCK // page
    c = pl.program_id(1)
    n_past = nb * MOBA_BLOCK

    @pl.when(c == 0)
    def _():
        q = q_ref[...]
        qt = jnp.concatenate([q] * H_A + [jnp.zeros((LANES - ncol, W_A), F32)], axis=0)
        r = lax.broadcasted_iota(jnp.int32, qt.shape, 0)
        hl = _lane_head(lax.broadcasted_iota(jnp.int32, qt.shape, 1))
        qrows_ref[...] = jnp.where((r >= hl * ts) & (r < hl * ts + ts), qt, 0.0)
        acc_ref[...] = jnp.zeros(acc_ref.shape, F32)

    @pl.when(c < nchunk)
    def _():
        qrb = qrows_ref[...].astype(BF16)
        for tt in range(pps):
            kp = k_pages[tt][...]
            g = c * pps + tt
            lg_ref[pl.ds(pl.multiple_of(g * page, page), page), :] = _dot_nt(kp.astype(BF16), qrb)
            ks = jnp.sum(kp, axis=0, keepdims=True)
            if tt % ppb == 0:
                ksum = ks
            else:
                ksum = ksum + ks
            if tt % ppb == ppb - 1:
                bidx = c * (pps // ppb) + tt // ppb
                km_ref[pl.ds(bidx, 1), :] = ksum * (1.0 / MOBA_BLOCK)

    @pl.when(c == nchunk - 1)
    def _():
        qr = qrows_ref[...]
        s = _dot_nt(km_ref[...], qr, precision=HIGHEST)
        sel_ref[...] = _top3_rows(s, None)
        kn = jnp.concatenate([kn_ref[...], jnp.zeros((page - ts, W_A), F32)], axis=0)
        lo = _dot_nt(kn.astype(BF16), qr.astype(BF16))
        krow = lax.broadcasted_iota(jnp.int32, lo.shape, 0)
        qcol = lax.broadcasted_iota(jnp.int32, lo.shape, 1) & (ts - 1)
        lo = jnp.where((krow < ts) & (krow <= qcol), lo, NEG)
        m0 = jnp.max(lo, axis=0, keepdims=True)

        def masked(j):
            blk = lg_ref[pl.ds(pl.multiple_of(j * MOBA_BLOCK, MOBA_BLOCK), MOBA_BLOCK), :]
            return jnp.where(sel_ref[pl.ds(j, 1), :] > 0.0, blk, NEG)

        m = lax.fori_loop(0, nb, lambda j, m: jnp.maximum(m, jnp.max(masked(j), axis=0, keepdims=True)), m0)
        p_own = jnp.exp(lo - m)
        lg_ref[n_past:n_past + page, :] = p_own

        def p2(j, l):
            p = jnp.exp(masked(j) - m)
            lg_ref[pl.ds(pl.multiple_of(j * MOBA_BLOCK, MOBA_BLOCK), MOBA_BLOCK), :] = p
            return l + jnp.sum(p, axis=0, keepdims=True)

        l = lax.fori_loop(0, nb, p2, jnp.sum(p_own, axis=0, keepdims=True))
        inv_ref[...] = jnp.broadcast_to(1.0 / l, inv_ref.shape)

    @pl.when(c >= nchunk)
    def _():
        inv = inv_ref[0:1, :]
        acc = acc_ref[...]
        for tt in range(pps):
            g = (c - nchunk) * pps + tt
            p = lg_ref[pl.ds(pl.multiple_of(g * page, page), page), :] * inv
            acc = acc + _dot(p.T.astype(BF16), v_pages[tt][...].astype(BF16))
        acc_ref[...] = acc

    @pl.when(c == 2 * nchunk - 1)
    def _():
        inv = inv_ref[0:1, :]
        p = lg_ref[n_past:n_past + page, :] * inv
        vn = jnp.concatenate([vn_ref[...], jnp.zeros((page - ts, W_A), F32)], axis=0)
        acc = acc_ref[...] + _dot(p.T.astype(BF16), vn.astype(BF16))
        a3 = acc[:ncol].reshape(H_A, ts, W_A)
        hrow = lax.broadcasted_iota(jnp.int32, a3.shape, 0)
        hl = _lane_head(lax.broadcasted_iota(jnp.int32, a3.shape, 2))
        o_ref[...] = jnp.sum(jnp.where(hrow == hl, a3, 0.0), axis=0).astype(o_ref.dtype)


def _moba_sample(qa, kn, vn, cache_k, cache_v, page_table, layer, ts):
    db, n_pages = page_table.shape
    page = cache_k.shape[2]
    nb = n_pages * page // MOBA_BLOCK
    pps = 8
    nchunk = n_pages // pps
    rows = pl.BlockSpec((ts, W_A), lambda b, c, pt: (b, 0))

    def kpage(tt):
        return pl.BlockSpec((None, None, page, W_A),
                            lambda b, c, pt: (layer, pt[b, jnp.minimum(c, nchunk - 1) * pps + tt], 0, 0))

    def vpage(tt):
        return pl.BlockSpec((None, None, page, W_A),
                            lambda b, c, pt: (layer, pt[b, jnp.maximum(c - nchunk, 0) * pps + tt], 0, 0))

    grid_spec = pltpu.PrefetchScalarGridSpec(
        num_scalar_prefetch=1,
        grid=(db, 2 * nchunk),
        in_specs=[rows, rows, rows] + [kpage(tt) for tt in range(pps)] + [vpage(tt) for tt in range(pps)],
        out_specs=rows,
        scratch_shapes=[pltpu.VMEM((nb * MOBA_BLOCK + page, LANES), F32),
                        pltpu.VMEM((nb, W_A), F32),
                        pltpu.VMEM((nb, LANES), F32),
                        pltpu.VMEM((8, LANES), F32),
                        pltpu.VMEM((LANES, W_A), F32),
                        pltpu.VMEM((LANES, W_A), F32)],
    )
    return pl.pallas_call(
        functools.partial(_moba_sample_kernel, pps=pps, nchunk=nchunk, page=page),
        grid_spec=grid_spec,
        out_shape=jax.ShapeDtypeStruct((db * ts, W_A), BF16 if ts % 16 == 0 else F32),
        compiler_params=_cparams(2),
        name="moba_sample",
    )(page_table, qa, kn, vn, *([cache_k] * pps), *([cache_v] * pps))


def _ret_chunk(q, k, v, s, lg, c_true):
    n = q.shape[0]
    ii = lax.broadcasted_iota(jnp.int32, (n, n), 0)
    jj = lax.broadcasted_iota(jnp.int32, (n, n), 1)
    diff = (ii - jj).astype(F32)
    causal = diff >= 0
    dmat = jnp.where(causal, jnp.exp(jnp.where(causal, diff * lg, 0.0)), 0.0)
    ic = lax.broadcasted_iota(jnp.int32, (n, 1), 0).astype(F32)
    q_dec = jnp.exp((ic + 1.0) * lg)
    k_dec = jnp.exp((c_true - 1.0 - ic) * lg)
    c_dec = jnp.exp(jnp.full((1, 1), float(c_true), F32) * lg)
    att = _dot_nt(q, k) * dmat
    inner = _dot(att.astype(BF16), v)
    cross = _dot(q, s.astype(BF16)) * q_dec
    kd = (k.astype(F32) * k_dec).T.astype(BF16)
    s_new = s * c_dec + _dot(kd, v)
    return inner + cross, s_new


def _head_norm_gate(o, g):
    on = o * lax.rsqrt(jnp.mean(o * o, axis=-1, keepdims=True) + EPS)
    return g.astype(F32) * on.astype(F32)


def _ret_prompt_kernel(lg_ref, q_ref, k_ref, v_ref, g_ref, r_ref, s_ref):
    h = pl.program_id(1)
    c = pl.program_id(2)

    @pl.when(c == 0)
    def _():
        s_ref[...] = jnp.zeros(s_ref.shape, F32)

    o, s_new = _ret_chunk(q_ref[...], k_ref[...], v_ref[...], s_ref[...], lg_ref[h], RET_CHUNK)
    s_ref[...] = s_new
    r_ref[...] = _head_norm_gate(o, g_ref[...]).astype(r_ref.dtype)


def _ret_prompt(log_g, qr, kr, vr, gr, batch, t):
    nc = t // RET_CHUNK
    qk = pl.BlockSpec((None, RET_CHUNK, DK_R), lambda b, h, c, lg: (b, c, h))
    vv = pl.BlockSpec((None, RET_CHUNK, DV_R), lambda b, h, c, lg: (b, c, h))
    grid_spec = pltpu.PrefetchScalarGridSpec(
        num_scalar_prefetch=1,
        grid=(batch, H_R, nc),
        in_specs=[qk, qk, vv, vv],
        out_specs=[vv, pl.BlockSpec((None, None, DK_R, DV_R), lambda b, h, c, lg: (b, h, 0, 0))],
    )
    r, s_fin = pl.pallas_call(
        _ret_prompt_kernel,
        grid_spec=grid_spec,
        out_shape=[jax.ShapeDtypeStruct((batch, t, W_RV), BF16),
                   jax.ShapeDtypeStruct((batch, H_R, DK_R, DV_R), F32)],
        compiler_params=_cparams(3),
        name="ret_prompt",
    )(log_g, qr.reshape(batch, t, W_RQK), kr.reshape(batch, t, W_RQK),
      vr.reshape(batch, t, W_RV), gr.reshape(batch, t, W_RV))
    return r.reshape(batch * t, W_RV), s_fin


def _ret_sample_kernel(lg_ref, q_ref, k_ref, v_ref, g_ref, s0_ref, r_ref, s_ref, *, ts, spb):
    h = pl.program_id(1)
    pad = lambda a: jnp.concatenate(
        [a, jnp.zeros((RET_CHUNK - ts, a.shape[1]), a.dtype)], axis=0).astype(BF16)
    outs = []
    for sidx in range(spb):
        sl = slice(sidx * ts, (sidx + 1) * ts)
        o, s_new = _ret_chunk(pad(q_ref[sl, :]), pad(k_ref[sl, :]), pad(v_ref[sl, :]),
                              s0_ref[sidx].astype(F32), lg_ref[h], ts)
        s_ref[sidx] = s_new.astype(s_ref.dtype)
        outs.append(_head_norm_gate(o[:ts], g_ref[sl, :]))
    r_ref[...] = jnp.concatenate(outs, axis=0).astype(r_ref.dtype)


def _ret_sample(log_g, qr, kr, vr, gr, state, layer, db, ts):
    spb = 2 if db % 2 == 0 else 1
    qk = pl.BlockSpec((spb * ts, DK_R), lambda b, h, lg: (b, h))
    vv = pl.BlockSpec((spb * ts, DV_R), lambda b, h, lg: (b, h))
    grid_spec = pltpu.PrefetchScalarGridSpec(
        num_scalar_prefetch=1,
        grid=(db // spb, H_R),
        in_specs=[qk, qk, vv, vv,
                  pl.BlockSpec((None, spb, None, DK_R, DV_R), lambda b, h, lg: (layer, b, h, 0, 0))],
        out_specs=[vv, pl.BlockSpec((spb, None, DK_R, DV_R), lambda b, h, lg: (b, h, 0, 0))],
    )
    return pl.pallas_call(
        functools.partial(_ret_sample_kernel, ts=ts, spb=spb),
        grid_spec=grid_spec,
        out_shape=[jax.ShapeDtypeStruct((db * ts, W_RV), BF16),
                   jax.ShapeDtypeStruct((db, H_R, DK_R, DV_R), state.dtype)],
        compiler_params=_cparams(2),
        name="ret_sample",
    )(log_g, qr, kr, vr, gr, state)


def _merge_kernel(x_ref, a_ref, r_ref, sga_ref, sgr_ref, wpa_ref, wpr_ref, wo_ref, g_ref, o_ref):
    merged = (sga_ref[...].astype(F32) * _dot(a_ref[...].astype(BF16), wpa_ref[...])
              + sgr_ref[...].astype(F32) * _dot(r_ref[...], wpr_ref[...]))
    y = _dot(merged.astype(BF16), wo_ref[...])
    o_ref[...] = x_ref[...] + _rms_scale(y, g_ref[...])


def _merge(x, a, r, sga, sgr, wpa, wpr, wo, g, tm):
    n, d = x.shape
    rows = lambda width: pl.BlockSpec((tm, width), lambda i: (i, 0))
    return pl.pallas_call(
        _merge_kernel,
        grid=(n // tm,),
        in_specs=[rows(d), rows(W_A), rows(W_RV), rows(d), rows(d),
                  _resident(wpa.shape), _resident(wpr.shape), _resident(wo.shape), _resident((1, d))],
        out_specs=rows(d),
        out_shape=jax.ShapeDtypeStruct((n, d), F32),
        compiler_params=_cparams(1),
        name="merge",
    )(x, a, r, sga, sgr, wpa, wpr, wo, g)


FF_CHUNK = 256


def _ffn_kernel(x_ref, gpre_ref, w1_ref, w2_ref, gpost_ref, o_ref):
    d_ff = w2_ref.shape[0]
    x = x_ref[...]
    h = _rms_scale(x, gpre_ref[...]).astype(BF16)
    y = jnp.zeros(x.shape, F32)
    for c0 in range(0, d_ff, FF_CHUNK):
        gt = _dot(h, w1_ref[:, c0:c0 + FF_CHUNK])
        up = _dot(h, w1_ref[:, d_ff + c0:d_ff + c0 + FF_CHUNK])
        act = (gt * jax.nn.sigmoid(gt) * up).astype(BF16)
        y = y + _dot(act, w2_ref[c0:c0 + FF_CHUNK, :])
    o_ref[...] = x + _rms_scale(y, gpost_ref[...])


def _ffn(x, gpre, w1, w2, gpost, tm):
    n, d = x.shape
    assert w2.shape[0] % FF_CHUNK == 0
    rows = pl.BlockSpec((tm, d), lambda i: (i, 0))
    return pl.pallas_call(
        _ffn_kernel,
        grid=(n // tm,),
        in_specs=[rows, _resident((1, d)), _resident(w1.shape), _resident(w2.shape), _resident((1, d))],
        out_specs=rows,
        out_shape=jax.ShapeDtypeStruct((n, d), F32),
        compiler_params=_cparams(1),
        name="ffn",
    )(x, gpre, w1, w2, gpost)


def _rope_tables(pos):
    p = pos.astype(F32)[:, None]
    n = pos.shape[0]
    fa = ROPE_THETA ** (-jnp.arange(0, ROT_DIM, 2, dtype=F32) / ROT_DIM)
    ang = p * fa[None, :]
    cos, sin = jnp.cos(ang), jnp.sin(ang)
    half = ROT_DIM // 2
    rest = DH_A - ROT_DIM
    z_h, z_r, o_r = jnp.zeros((n, half), F32), jnp.zeros((n, rest), F32), jnp.ones((n, rest), F32)
    two = lambda a: jnp.concatenate([a, a], axis=1)
    ca = two(jnp.concatenate([cos, cos, o_r], axis=1))
    s1 = two(jnp.concatenate([z_h, sin, z_r], axis=1))
    s2 = two(jnp.concatenate([-sin, z_h, z_r], axis=1))
    fr = 1.0 / (RET_THETA ** jnp.linspace(0.0, 1.0, DK_R // 2, dtype=F32))
    angr = p * fr[None, :]
    cr = two(jnp.cos(angr))
    sr = jnp.concatenate([-jnp.sin(angr), jnp.sin(angr)], axis=1)
    return ca, s1, s2, cr, sr


def _row_tile(n, cap):
    tm = min(n, cap)
    while n % tm:
        tm //= 2
    return tm


def kernel(x_prompt, x_sample, cache_k, cache_v, state_ret, page_table, g_mix_pre, w_in, w_proj_attn, w_proj_ret, w_out, g_mix_post, g_ffn_pre, w_ffn_in, w_ffn_out, g_ffn_post):
    batch, t, d = x_prompt.shape
    db, ts, _ = x_sample.shape
    depth = w_in.shape[0]
    n_pool, page = cache_k.shape[1], cache_k.shape[2]
    n_pages = page_table.shape[1]
    past_len = n_pages * page
    assert cache_k.shape[3] * cache_k.shape[4] == W_A
    assert t % MOBA_BLOCK == 0 and t % RET_CHUNK == 0 and past_len % MOBA_BLOCK == 0
    assert ts < RET_CHUNK and ts % 8 == 0 and ts & (ts - 1) == 0 and H_A * ts <= LANES
    assert MOBA_BLOCK % page == 0 and DH_A & (DH_A - 1) == 0
    assert n_pages % 8 == 0

    tabs_p = _rope_tables(jnp.arange(t, dtype=jnp.int32))
    tabs_s = tuple(jnp.tile(tb, (db, 1)) for tb in _rope_tables(past_len + jnp.arange(ts, dtype=jnp.int32)))
    log_g = jnp.log(1.0 - 2.0 ** (-5.0 - jnp.arange(H_R, dtype=F32)))
    ck = cache_k.reshape(cache_k.shape[0], n_pool, page, W_A)
    cv = cache_v.reshape(cache_v.shape[0], n_pool, page, W_A)

    np_, ns_ = batch * t, db * ts
    tm_p = _row_tile(t, 512)
    tm_s = _row_tile(ns_, 512)
    xp = x_prompt.reshape(np_, d)
    xs = x_sample.reshape(ns_, d)
    row = lambda v: v.reshape(1, d)

    kp, vp, sp, kss, vss, sss = [], [], [], [], [], []
    for l in range(depth):
        w1 = w_in[l].astype(BF16)
        wpa, wpr, wo = w_proj_attn[l].astype(BF16), w_proj_ret[l].astype(BF16), w_out[l].astype(BF16)
        wf1, wf2 = w_ffn_in[l].astype(BF16), w_ffn_out[l].astype(BF16)

        qa, ka, va, kb, vb, qr, kr, vr, gr, sga, sgr = _in_proj(
            xp, row(g_mix_pre[l]), w1, tabs_p, tm_p, t // tm_p)
        a = _moba_prompt(qa, kb, vb, batch, t)
        r, s_fin = _ret_prompt(log_g, qr, kr, vr, gr, batch, t)
        xp = _merge(xp, a, r, sga, sgr, wpa, wpr, wo, row(g_mix_post[l]), tm_p)
        xp = _ffn(xp, row(g_ffn_pre[l]), wf1, wf2, row(g_ffn_post[l]), tm_p)
        kp.append(ka.reshape(batch, t, H_A, DH_A))
        vp.append(va.reshape(batch, t, H_A, DH_A))
        sp.append(s_fin)

        qa, ka, va, kb, vb, qr, kr, vr, gr, sga, sgr = _in_proj(
            xs, row(g_mix_pre[l]), w1, tabs_s, tm_s, ns_ // tm_s)
        f = lambda v: v.astype(F32)
        a = _moba_sample(f(qa), f(kb), f(vb), ck, cv, page_table, l, ts)
        r, s_new = _ret_sample(log_g, f(qr), f(kr), f(vr), f(gr), state_ret, l, db, ts)
        xs = _merge(xs, a, r, sga, sgr, wpa, wpr, wo, row(g_mix_post[l]), tm_s)
        xs = _ffn(xs, row(g_ffn_pre[l]), wf1, wf2, row(g_ffn_post[l]), tm_s)
        kss.append(ka.reshape(db, ts, H_A, DH_A))
        vss.append(va.reshape(db, ts, H_A, DH_A))
        sss.append(s_new)

    return (xp.reshape(batch, t, d), xs.reshape(db, ts, d), jnp.stack(kp), jnp.stack(vp), jnp.stack(sp),
            jnp.stack(kss), jnp.stack(vss), jnp.stack(sss))
```

```python
import functools
import math

import jax
import jax.numpy as jnp
from jax import lax
from jax.experimental import pallas as pl
from jax.experimental.pallas import tpu as pltpu

F32 = jnp.float32
BF16 = jnp.bfloat16

H_A = 8
DH_A = 64
MOBA_BLOCK = 256
MOBA_TOPK = 3
ROPE_THETA = 500000.0
ROT_DIM = DH_A // 4
H_R = 4
DK_R = 128
DV_R = 256
RET_CHUNK = 128
RET_THETA = 10000.0
EPS = 1e-6
NEG = -1e30
BIG = 1e30
W_A = H_A * DH_A
W_RQK = H_R * DK_R
W_RV = H_R * DV_R
Q_SCALE = DH_A ** -0.5 * math.log2(math.e)

LANES = 128
BF16_ROWS = 16
VMEM_LIMIT = 56 * 1024 * 1024
PV_ROWS = DH_A + BF16_ROWS
QK_GROUP = 4
PV_GROUP = 8


def _cparams(n_grid):
    return pltpu.CompilerParams(dimension_semantics=("arbitrary",) * n_grid,
                                vmem_limit_bytes=VMEM_LIMIT)


def _resident(shape):
    return pl.BlockSpec(shape, lambda *_: (0,) * len(shape), pipeline_mode=pl.Buffered(1))


def _rms_scale(y, g):
    return y * lax.rsqrt(jnp.mean(y * y, axis=-1, keepdims=True) + EPS) * g


def _dot(a, b):
    return jnp.dot(a, b, preferred_element_type=F32)


def _dot_nt(a, b):
    return lax.dot_general(a, b, (((1,), (1,)), ((), ())), preferred_element_type=F32)


def _in_proj_kernel(x_ref, g_ref, w_ref, ca_ref, s1_ref, s2_ref, cr_ref, sr_ref,
                    qa_ref, ka_ref, va_ref, kb_ref, vb_ref, qr_ref, kr_ref, vr_ref,
                    gr_ref, sga_ref, sgr_ref):
    d_model = x_ref.shape[1]
    h = _rms_scale(x_ref[...], g_ref[...]).astype(BF16)

    def proj(c0, n):
        return _dot(h, w_ref[:, c0:c0 + n])

    ca, s1, s2 = ca_ref[...], s1_ref[...], s2_ref[...]
    cr, sr = cr_ref[...], sr_ref[...]

    def rope_a(z):
        outs = []
        for c in range(z.shape[1] // LANES):
            zc = z[:, c * LANES:(c + 1) * LANES]
            outs.append(zc * ca + pltpu.roll(zc, ROT_DIM // 2, 1) * s1
                        + pltpu.roll(zc, LANES - ROT_DIM // 2, 1) * s2)
        return jnp.concatenate(outs, axis=1)

    def rope_r(z):
        outs = []
        for c in range(z.shape[1] // LANES):
            zc = z[:, c * LANES:(c + 1) * LANES]
            outs.append(zc * cr + pltpu.roll(zc, DK_R // 2, 1) * sr)
        return jnp.concatenate(outs, axis=1)

    c0 = 0
    qa_ref[...] = (rope_a(proj(c0, W_A)) * Q_SCALE).astype(qa_ref.dtype)
    c0 += W_A
    ka = rope_a(proj(c0, W_A))
    ka_ref[...] = ka
    kb_ref[...] = ka.astype(BF16)
    c0 += W_A
    va = proj(c0, W_A)
    va_ref[...] = va
    vb_ref[...] = va.astype(BF16)
    c0 += W_A
    qr_ref[...] = rope_r(proj(c0, W_RQK)).astype(qr_ref.dtype)
    c0 += W_RQK
    kr_ref[...] = (rope_r(proj(c0, W_RQK)) * (DK_R ** -0.5)).astype(kr_ref.dtype)
    c0 += W_RQK
    vr_ref[...] = proj(c0, W_RV).astype(vr_ref.dtype)
    c0 += W_RV
    gr = proj(c0, W_RV)
    gr_ref[...] = (gr * jax.nn.sigmoid(gr)).astype(gr_ref.dtype)
    c0 += W_RV
    sga_ref[...] = jax.nn.sigmoid(proj(c0, d_model)).astype(sga_ref.dtype)
    c0 += d_model
    sgr_ref[...] = jax.nn.sigmoid(proj(c0, d_model)).astype(sgr_ref.dtype)


def _in_proj(x, g, w, tabs, tm, n_tab_blocks):
    n, d = x.shape
    d_in = w.shape[1]
    rows = lambda width: pl.BlockSpec((tm, width), lambda i: (i, 0))
    tab = pl.BlockSpec((tm, LANES), lambda i: (i % n_tab_blocks, 0))
    widths = [W_A, W_A, W_A, W_A, W_A, W_RQK, W_RQK, W_RV, W_RV, d, d]
    dtypes = [BF16, F32, F32, BF16, BF16, BF16, BF16, BF16, BF16, BF16, BF16]
    return pl.pallas_call(
        _in_proj_kernel,
        grid=(n // tm,),
        in_specs=[rows(d), _resident((1, d)), _resident((d, d_in))] + [tab] * 5,
        out_specs=[rows(wd) for wd in widths],
        out_shape=[jax.ShapeDtypeStruct((n, wd), dt) for wd, dt in zip(widths, dtypes)],
        compiler_params=_cparams(1),
        name="in_proj",
    )(x, g, w, *tabs)


def _top3(s, axis):
    n = s.shape[axis]
    idx = lax.broadcasted_iota(jnp.int32, s.shape, axis).astype(F32)
    sel = jnp.zeros(s.shape, F32)
    for _ in range(MOBA_TOPK):
        m = jnp.max(s, axis=axis, keepdims=True)
        first = jnp.min(jnp.where(s == m, idx, float(n)), axis=axis, keepdims=True)
        pick = idx == first
        sel = jnp.where(pick, 1.0, sel)
        s = jnp.where(pick, -jnp.inf, s)
    return sel


def _moba_prompt_kernel(q_ref, k_ref, v_ref, o_ref, vt_ref, lg_ref, cmax_ref, csum_ref, muse_ref, acc_ref):
    nb = k_ref.shape[0]
    nq = q_ref.shape[0]
    i = pl.program_id(2)

    @pl.when(i == 0)
    def _():
        cmax_ref[...] = jnp.zeros(cmax_ref.shape, F32)
        csum_ref[...] = jnp.zeros(csum_ref.shape, F32)
        pad = jnp.concatenate([jnp.ones((1, MOBA_BLOCK), F32),
                               jnp.zeros((PV_ROWS - DH_A - 1, MOBA_BLOCK), F32)], axis=0)
        for j in range(nb):
            vt = v_ref[j].astype(F32).T
            for hh in range(2):
                vt_ref[j, hh] = jnp.concatenate([vt[hh * DH_A:(hh + 1) * DH_A], pad], axis=0).astype(BF16)

    qt = q_ref[...].astype(F32).T
    drow = lax.broadcasted_iota(jnp.int32, qt.shape, 0)
    qtb = [jnp.where(drow < DH_A, qt, 0.0).astype(BF16), jnp.where(drow >= DH_A, qt, 0.0).astype(BF16)]

    def logits_block(j):
        kblk = k_ref[j]
        for hh in range(2):
            lg = _dot(kblk, qtb[hh])
            lg_ref[hh, j] = lg
            cmax_ref[hh, pl.ds(j, 1), :] = jnp.max(lg, axis=0, keepdims=True)
            csum_ref[hh, pl.ds(j, 1), :] = jnp.sum(lg, axis=0, keepdims=True)

    def pass1(t, carry):
        for g in range(QK_GROUP):
            logits_block(t * QK_GROUP + g)
        return carry

    lax.fori_loop(0, (i + QK_GROUP - 1) // QK_GROUP, pass1, 0)

    kpos = lax.broadcasted_iota(jnp.int32, (MOBA_BLOCK, nq), 0)
    qpos = lax.broadcasted_iota(jnp.int32, (MOBA_BLOCK, nq), 1)
    blk = lax.broadcasted_iota(jnp.int32, (nb, nq), 0)
    kown = k_ref[i]
    for hh in range(2):
        lg = jnp.where(kpos <= qpos, _dot(kown, qtb[hh]), NEG)
        lg_ref[hh, i] = lg
        m = jnp.max(lg, axis=0, keepdims=True)
        sel = _top3(jnp.where(blk < i, csum_ref[hh], -jnp.inf), 0)
        sel = (sel > 0.0) & (blk < i)
        m = jnp.maximum(m, jnp.max(jnp.where(sel, cmax_ref[hh], NEG), axis=0, keepdims=True))
        muse_ref[hh] = jnp.where(sel | (blk == i), m, BIG)
        acc_ref[hh] = jnp.zeros(acc_ref.shape[1:], F32)

    def pass2(t, carry):
        for hh in range(2):
            ps, vs = [], []
            for g in range(PV_GROUP):
                jj = t * PV_GROUP + g
                jc = jnp.minimum(jj, i)
                x = lg_ref[hh, jc] - muse_ref[hh, pl.ds(jj, 1), :]
                ps.append(jnp.exp2(x.astype(BF16)))
                vs.append(vt_ref[jc, hh])
            acc_ref[hh] += _dot(jnp.concatenate(vs, axis=1), jnp.concatenate(ps, axis=0))
        return carry

    lax.fori_loop(0, (i + PV_GROUP) // PV_GROUP, pass2, 0)
    outs = []
    for hh in range(2):
        acc = acc_ref[hh]
        outs.append(acc[:DH_A] / acc[DH_A:DH_A + 1])
    o_ref[...] = jnp.concatenate(outs, axis=0).T.astype(o_ref.dtype)


def _moba_prompt(qa, kb, vb, batch, t):
    nb = t // MOBA_BLOCK
    assert nb % PV_GROUP == 0 and nb % QK_GROUP == 0
    npair = W_A // LANES
    q3 = qa.reshape(batch, t, W_A)
    k4 = kb.reshape(batch, nb, MOBA_BLOCK, W_A)
    v4 = vb.reshape(batch, nb, MOBA_BLOCK, W_A)
    kv_spec = pl.BlockSpec((None, nb, MOBA_BLOCK, LANES), lambda b, hp, i: (b, 0, 0, hp))
    q_spec = pl.BlockSpec((None, MOBA_BLOCK, LANES), lambda b, hp, i: (b, i, hp))
    out = pl.pallas_call(
        _moba_prompt_kernel,
        grid=(batch, npair, nb),
        in_specs=[q_spec, kv_spec, kv_spec],
        out_specs=q_spec,
        out_shape=jax.ShapeDtypeStruct((batch, t, W_A), BF16),
        scratch_shapes=[pltpu.VMEM((nb, 2, PV_ROWS, MOBA_BLOCK), BF16),
                        pltpu.VMEM((2, nb, MOBA_BLOCK, MOBA_BLOCK), F32),
                        pltpu.VMEM((2, nb, MOBA_BLOCK), F32),
                        pltpu.VMEM((2, nb, MOBA_BLOCK), F32),
                        pltpu.VMEM((2, nb, MOBA_BLOCK), F32),
                        pltpu.VMEM((2, PV_ROWS, MOBA_BLOCK), F32)],
        compiler_params=_cparams(3),
        name="moba_prompt",
    )(q3, k4, v4)
    return out.reshape(batch * t, W_A)


def _lane_head(lane):
    return lax.shift_right_logical(lane, DH_A.bit_length() - 1)


def _moba_sample_kernel(pt_ref, q_ref, kn_ref, vn_ref, *rest, pps, nchunk, page):
    del pt_ref
    k_pages = rest[:pps]
    v_pages = rest[pps:2 * pps]
    o_ref = rest[2 * pps]
    lg_ref, bsum_ref, bmax_ref, muse_ref, acc_ref, lsum_ref, qrows_ref = rest[2 * pps + 1:]
    ts = q_ref.shape[0]
    nrow = H_A * ts
    n_pages = pps * nchunk
    ppb = MOBA_BLOCK // page
    nb = n_pages // ppb
    c = pl.program_id(1)
    lane = lax.broadcasted_iota(jnp.int32, (nrow, LANES), 1)

    @pl.when(c == 0)
    def _():
        qt = jnp.concatenate([q_ref[...]] * H_A, axis=0)
        r = lax.broadcasted_iota(jnp.int32, qt.shape, 0)
        hl = _lane_head(lax.broadcasted_iota(jnp.int32, qt.shape, 1))
        qrows_ref[...] = jnp.where((r >= hl * ts) & (r < hl * ts + ts), qt, 0.0).astype(BF16)
        bsum_ref[...] = jnp.full(bsum_ref.shape, -jnp.inf, F32)
        bmax_ref[...] = jnp.full(bmax_ref.shape, NEG, F32)
        acc_ref[...] = jnp.zeros(acc_ref.shape, F32)
        lsum_ref[...] = jnp.zeros(lsum_ref.shape, F32)

    @pl.when(c < nchunk)
    def _():
        qrb = qrows_ref[...]
        bsum, bmax = bsum_ref[...], bmax_ref[...]
        for tt in range(pps):
            ktp = k_pages[tt][...].reshape(W_A, page).astype(BF16)
            lg = _dot(qrb, ktp)
            lg_ref[c * pps + tt] = lg
            if tt % ppb == 0:
                s_acc, m_acc = lg, lg
            else:
                s_acc, m_acc = s_acc + lg, jnp.maximum(m_acc, lg)
            if tt % ppb == ppb - 1:
                here = lane == c * (pps // ppb) + tt // ppb
                bsum = jnp.where(here, jnp.sum(s_acc, axis=1, keepdims=True), bsum)
                bmax = jnp.where(here, jnp.max(m_acc, axis=1, keepdims=True), bmax)
        bsum_ref[...] = bsum
        bmax_ref[...] = bmax

    @pl.when(c == nchunk - 1)
    def _():
        sel = (_top3(bsum_ref[...], 1) > 0.0) & (lane < nb)
        kn = jnp.concatenate([kn_ref[...], jnp.zeros((page - ts, W_A), F32)], axis=0).astype(BF16)
        lo = _dot_nt(qrows_ref[...], kn)
        kpos = lax.broadcasted_iota(jnp.int32, lo.shape, 1)
        qidx = lax.broadcasted_iota(jnp.int32, lo.shape, 0) & (ts - 1)
        lo = jnp.where((kpos < ts) & (kpos <= qidx), lo, NEG)
        lg_ref[n_pages] = lo
        m = jnp.maximum(jnp.max(lo, axis=1, keepdims=True),
                        jnp.max(jnp.where(sel, bmax_ref[...], NEG), axis=1, keepdims=True))
        muse_ref[...] = jnp.where(sel | (lane == nb), m, BIG)

    def attend(g, j, vt_b):
        mj = jnp.sum(jnp.where(lane == j, muse_ref[...], 0.0), axis=1, keepdims=True)
        p = jnp.exp2((lg_ref[g] - mj).astype(BF16))
        lsum_ref[...] += p.astype(F32)
        acc_ref[...] += _dot_nt(p, vt_b)

    @pl.when(c >= nchunk)
    def _():
        for tt in range(pps):
            g = (c - nchunk) * pps + tt
            attend(g, (c - nchunk) * (pps // ppb) + tt // ppb, v_pages[tt][...].reshape(W_A, page).astype(BF16))

    @pl.when(c == 2 * nchunk - 1)
    def _():
        vn = jnp.concatenate([vn_ref[...], jnp.zeros((page - ts, W_A), F32)], axis=0)
        attend(n_pages, nb, vn.T.astype(BF16))
        acc = acc_ref[...] / jnp.sum(lsum_ref[...], axis=1, keepdims=True)
        a3 = acc.reshape(H_A, ts, W_A)
        hrow = lax.broadcasted_iota(jnp.int32, a3.shape, 0)
        hl = _lane_head(lax.broadcasted_iota(jnp.int32, a3.shape, 2))
        o_ref[...] = jnp.sum(jnp.where(hrow == hl, a3, 0.0), axis=0).astype(o_ref.dtype)


def _moba_sample(qa, kn, vn, cache_kt, cache_vt, page_table, layer, ts):
    db, n_pages = page_table.shape
    page = cache_kt.shape[4]
    assert page == LANES and n_pages * page // MOBA_BLOCK < LANES
    pps = min(16, n_pages)
    nchunk = n_pages // pps
    assert n_pages % pps == 0 and pps % (MOBA_BLOCK // page) == 0
    nrow = H_A * ts
    rows = pl.BlockSpec((ts, W_A), lambda b, c, pt: (b, 0))

    def kpage(tt):
        return pl.BlockSpec((None, None, H_A, DH_A, page),
                            lambda b, c, pt: (layer, pt[b, jnp.minimum(c, nchunk - 1) * pps + tt], 0, 0, 0))

    def vpage(tt):
        return pl.BlockSpec((None, None, H_A, DH_A, page),
                            lambda b, c, pt: (layer, pt[b, jnp.maximum(c - nchunk, 0) * pps + tt], 0, 0, 0))

    grid_spec = pltpu.PrefetchScalarGridSpec(
        num_scalar_prefetch=1,
        grid=(db, 2 * nchunk),
        in_specs=[rows, rows, rows] + [kpage(tt) for tt in range(pps)] + [vpage(tt) for tt in range(pps)],
        out_specs=rows,
        scratch_shapes=[pltpu.VMEM((n_pages + 1, nrow, page), F32),
                        pltpu.VMEM((nrow, LANES), F32),
                        pltpu.VMEM((nrow, LANES), F32),
                        pltpu.VMEM((nrow, LANES), F32),
                        pltpu.VMEM((nrow, W_A), F32),
                        pltpu.VMEM((nrow, page), F32),
                        pltpu.VMEM((nrow, W_A), BF16)],
    )
    return pl.pallas_call(
        functools.partial(_moba_sample_kernel, pps=pps, nchunk=nchunk, page=page),
        grid_spec=grid_spec,
        out_shape=jax.ShapeDtypeStruct((db * ts, W_A), F32),
        compiler_params=_cparams(2),
        name="moba_sample",
    )(page_table, qa, kn, vn, *([cache_kt] * pps), *([cache_vt] * pps))


def _ret_chunk(q, k, v, s, lg, c_true):
    n = q.shape[0]
    ii = lax.broadcasted_iota(jnp.int32, (n, n), 0)
    jj = lax.broadcasted_iota(jnp.int32, (n, n), 1)
    diff = (ii - jj).astype(F32)
    causal = diff >= 0
    dmat = jnp.where(causal, jnp.exp(jnp.where(causal, diff * lg, 0.0)), 0.0)
    ic = lax.broadcasted_iota(jnp.int32, (n, 1), 0).astype(F32)
    q_dec = jnp.exp((ic + 1.0) * lg)
    k_dec = jnp.exp((c_true - 1.0 - ic) * lg)
    c_dec = jnp.exp(jnp.full((1, 1), float(c_true), F32) * lg)
    att = _dot_nt(q, k) * dmat
    inner = _dot(att.astype(BF16), v)
    cross = _dot(q, s.astype(BF16)) * q_dec
    kd = (k.astype(F32) * k_dec).T.astype(BF16)
    s_new = s * c_dec + _dot(kd, v)
    return inner + cross, s_new


def _head_norm_gate(o, g):
    on = o * lax.rsqrt(jnp.mean(o * o, axis=-1, keepdims=True) + EPS)
    return g.astype(F32) * on.astype(F32)


def _ret_prompt_kernel(lg_ref, q_ref, k_ref, v_ref, g_ref, r_ref, s_ref):
    c = pl.program_id(1)

    @pl.when(c == 0)
    def _():
        s_ref[...] = jnp.zeros(s_ref.shape, F32)

    for h in range(H_R):
        qk = slice(h * DK_R, (h + 1) * DK_R)
        vv = slice(h * DV_R, (h + 1) * DV_R)
        o, s_new = _ret_chunk(q_ref[:, qk], k_ref[:, qk], v_ref[:, vv], s_ref[h], lg_ref[h], RET_CHUNK)
        s_ref[h] = s_new
        r_ref[:, vv] = _head_norm_gate(o, g_ref[:, vv]).astype(r_ref.dtype)


def _ret_prompt(log_g, qr, kr, vr, gr, batch, t):
    nc = t // RET_CHUNK
    qk = pl.BlockSpec((None, RET_CHUNK, W_RQK), lambda b, c, lg: (b, c, 0))
    vv = pl.BlockSpec((None, RET_CHUNK, W_RV), lambda b, c, lg: (b, c, 0))
    grid_spec = pltpu.PrefetchScalarGridSpec(
        num_scalar_prefetch=1,
        grid=(batch, nc),
        in_specs=[qk, qk, vv, vv],
        out_specs=[vv, pl.BlockSpec((None, H_R, DK_R, DV_R), lambda b, c, lg: (b, 0, 0, 0))],
    )
    r, s_fin = pl.pallas_call(
        _ret_prompt_kernel,
        grid_spec=grid_spec,
        out_shape=[jax.ShapeDtypeStruct((batch, t, W_RV), BF16),
                   jax.ShapeDtypeStruct((batch, H_R, DK_R, DV_R), F32)],
        compiler_params=_cparams(2),
        name="ret_prompt",
    )(log_g, qr.reshape(batch, t, W_RQK), kr.reshape(batch, t, W_RQK),
      vr.reshape(batch, t, W_RV), gr.reshape(batch, t, W_RV))
    return r.reshape(batch * t, W_RV), s_fin


def _ret_sample_kernel(lg_ref, q_ref, k_ref, v_ref, g_ref, s0_ref, r_ref, s_ref, *, ts, spb):
    h = pl.program_id(1)
    pad = lambda a: jnp.concatenate(
        [a, jnp.zeros((RET_CHUNK - ts, a.shape[1]), a.dtype)], axis=0).astype(BF16)
    outs = []
    for sidx in range(spb):
        sl = slice(sidx * ts, (sidx + 1) * ts)
        o, s_new = _ret_chunk(pad(q_ref[sl, :]), pad(k_ref[sl, :]), pad(v_ref[sl, :]),
                              s0_ref[sidx].astype(F32), lg_ref[h], ts)
        s_ref[sidx] = s_new.astype(s_ref.dtype)
        outs.append(_head_norm_gate(o[:ts], g_ref[sl, :]))
    r_ref[...] = jnp.concatenate(outs, axis=0).astype(r_ref.dtype)


def _ret_sample(log_g, qr, kr, vr, gr, state, layer, db, ts):
    spb = 2 if db % 2 == 0 else 1
    qk = pl.BlockSpec((spb * ts, DK_R), lambda b, h, lg: (b, h))
    vv = pl.BlockSpec((spb * ts, DV_R), lambda b, h, lg: (b, h))
    grid_spec = pltpu.PrefetchScalarGridSpec(
        num_scalar_prefetch=1,
        grid=(db // spb, H_R),
        in_specs=[qk, qk, vv, vv,
                  pl.BlockSpec((None, spb, None, DK_R, DV_R), lambda b, h, lg: (layer, b, h, 0, 0))],
        out_specs=[vv, pl.BlockSpec((spb, None, DK_R, DV_R), lambda b, h, lg: (b, h, 0, 0))],
    )
    return pl.pallas_call(
        functools.partial(_ret_sample_kernel, ts=ts, spb=spb),
        grid_spec=grid_spec,
        out_shape=[jax.ShapeDtypeStruct((db * ts, W_RV), BF16),
                   jax.ShapeDtypeStruct((db, H_R, DK_R, DV_R), state.dtype)],
        compiler_params=_cparams(2),
        name="ret_sample",
    )(log_g, qr, kr, vr, gr, state)


def _merge_kernel(x_ref, a_ref, r_ref, sga_ref, sgr_ref, wpa_ref, wpr_ref, wo_ref, g_ref, o_ref):
    merged = (sga_ref[...].astype(F32) * _dot(a_ref[...].astype(BF16), wpa_ref[...])
              + sgr_ref[...].astype(F32) * _dot(r_ref[...], wpr_ref[...]))
    y = _dot(merged.astype(BF16), wo_ref[...])
    o_ref[...] = x_ref[...] + _rms_scale(y, g_ref[...])


def _merge(x, a, r, sga, sgr, wpa, wpr, wo, g, tm):
    n, d = x.shape
    rows = lambda width: pl.BlockSpec((tm, width), lambda i: (i, 0))
    return pl.pallas_call(
        _merge_kernel,
        grid=(n // tm,),
        in_specs=[rows(d), rows(W_A), rows(W_RV), rows(d), rows(d),
                  _resident(wpa.shape), _resident(wpr.shape), _resident(wo.shape), _resident((1, d))],
        out_specs=rows(d),
        out_shape=jax.ShapeDtypeStruct((n, d), F32),
        compiler_params=_cparams(1),
        name="merge",
    )(x, a, r, sga, sgr, wpa, wpr, wo, g)


FF_CHUNK = 256


def _ffn_kernel(x_ref, gpre_ref, w1_ref, w2_ref, gpost_ref, o_ref):
    d_ff = w2_ref.shape[0]
    x = x_ref[...]
    h = _rms_scale(x, gpre_ref[...]).astype(BF16)
    y = jnp.zeros(x.shape, F32)
    for c0 in range(0, d_ff, FF_CHUNK):
        gt = _dot(h, w1_ref[:, c0:c0 + FF_CHUNK])
        up = _dot(h, w1_ref[:, d_ff + c0:d_ff + c0 + FF_CHUNK])
        act = (gt * jax.nn.sigmoid(gt) * up).astype(BF16)
        y = y + _dot(act, w2_ref[c0:c0 + FF_CHUNK, :])
    o_ref[...] = x + _rms_scale(y, gpost_ref[...])


def _ffn(x, gpre, w1, w2, gpost, tm):
    n, d = x.shape
    assert w2.shape[0] % FF_CHUNK == 0
    rows = pl.BlockSpec((tm, d), lambda i: (i, 0))
    return pl.pallas_call(
        _ffn_kernel,
        grid=(n // tm,),
        in_specs=[rows, _resident((1, d)), _resident(w1.shape), _resident(w2.shape), _resident((1, d))],
        out_specs=rows,
        out_shape=jax.ShapeDtypeStruct((n, d), F32),
        compiler_params=_cparams(1),
        name="ffn",
    )(x, gpre, w1, w2, gpost)


def _rope_tables(pos):
    p = pos.astype(F32)[:, None]
    n = pos.shape[0]
    fa = ROPE_THETA ** (-jnp.arange(0, ROT_DIM, 2, dtype=F32) / ROT_DIM)
    ang = p * fa[None, :]
    cos, sin = jnp.cos(ang), jnp.sin(ang)
    half = ROT_DIM // 2
    rest = DH_A - ROT_DIM
    z_h, z_r, o_r = jnp.zeros((n, half), F32), jnp.zeros((n, rest), F32), jnp.ones((n, rest), F32)
    two = lambda a: jnp.concatenate([a, a], axis=1)
    ca = two(jnp.concatenate([cos, cos, o_r], axis=1))
    s1 = two(jnp.concatenate([z_h, sin, z_r], axis=1))
    s2 = two(jnp.concatenate([-sin, z_h, z_r], axis=1))
    fr = 1.0 / (RET_THETA ** jnp.linspace(0.0, 1.0, DK_R // 2, dtype=F32))
    angr = p * fr[None, :]
    cr = two(jnp.cos(angr))
    sr = jnp.concatenate([-jnp.sin(angr), jnp.sin(angr)], axis=1)
    return ca, s1, s2, cr, sr


def _row_tile(n, cap):
    tm = min(n, cap)
    while n % tm:
        tm //= 2
    return tm


def kernel(x_prompt, x_sample, cache_k, cache_v, state_ret, page_table, g_mix_pre, w_in, w_proj_attn, w_proj_ret, w_out, g_mix_post, g_ffn_pre, w_ffn_in, w_ffn_out, g_ffn_post):
    batch, t, d = x_prompt.shape
    db, ts, _ = x_sample.shape
    depth = w_in.shape[0]
    page = cache_k.shape[2]
    n_pages = page_table.shape[1]
    past_len = n_pages * page
    assert cache_k.shape[3:] == (H_A, DH_A)
    assert t % MOBA_BLOCK == 0 and t % RET_CHUNK == 0 and past_len % MOBA_BLOCK == 0
    assert ts < RET_CHUNK and ts % 8 == 0 and ts & (ts - 1) == 0 and H_A * ts <= LANES
    assert MOBA_BLOCK % page == 0 and DH_A & (DH_A - 1) == 0

    tabs_p = _rope_tables(jnp.arange(t, dtype=jnp.int32))
    tabs_s = tuple(jnp.tile(tb, (db, 1)) for tb in _rope_tables(past_len + jnp.arange(ts, dtype=jnp.int32)))
    log_g = jnp.log(1.0 - 2.0 ** (-5.0 - jnp.arange(H_R, dtype=F32)))
    ckt = cache_k.transpose(0, 1, 3, 4, 2)
    cvt = cache_v.transpose(0, 1, 3, 4, 2)

    np_, ns_ = batch * t, db * ts
    tm_p = _row_tile(t, 512)
    tm_s = _row_tile(ns_, 512)
    xp = x_prompt.reshape(np_, d)
    xs = x_sample.reshape(ns_, d)
    row = lambda v: v.reshape(1, d)

    kp, vp, sp, kss, vss, sss = [], [], [], [], [], []
    for l in range(depth):
        w1 = w_in[l].astype(BF16)
        wpa, wpr, wo = w_proj_attn[l].astype(BF16), w_proj_ret[l].astype(BF16), w_out[l].astype(BF16)
        wf1, wf2 = w_ffn_in[l].astype(BF16), w_ffn_out[l].astype(BF16)

        qa, ka, va, kb, vb, qr, kr, vr, gr, sga, sgr = _in_proj(
            xp, row(g_mix_pre[l]), w1, tabs_p, tm_p, t // tm_p)
        a = _moba_prompt(qa, kb, vb, batch, t)
        r, s_fin = _ret_prompt(log_g, qr, kr, vr, gr, batch, t)
        xp = _merge(xp, a, r, sga, sgr, wpa, wpr, wo, row(g_mix_post[l]), tm_p)
        xp = _ffn(xp, row(g_ffn_pre[l]), wf1, wf2, row(g_ffn_post[l]), tm_p)
        kp.append(ka.reshape(batch, t, H_A, DH_A))
        vp.append(va.reshape(batch, t, H_A, DH_A))
        sp.append(s_fin)

        qa, ka, va, kb, vb, qr, kr, vr, gr, sga, sgr = _in_proj(
            xs, row(g_mix_pre[l]), w1, tabs_s, tm_s, ns_ // tm_s)
        f = lambda v: v.astype(F32)
        a = _moba_sample(f(qa), f(kb), f(vb), ckt, cvt, page_table, l, ts)
        r, s_new = _ret_sample(log_g, f(qr), f(kr), f(vr), f(gr), state_ret, l, db, ts)
        xs = _merge(xs, a, r, sga, sgr, wpa, wpr, wo, row(g_mix_post[l]), tm_s)
        xs = _ffn(xs, row(g_ffn_pre[l]), wf1, wf2, row(g_ffn_post[l]), tm_s)
        kss.append(ka.reshape(db, ts, H_A, DH_A))
        vss.append(va.reshape(db, ts, H_A, DH_A))
        sss.append(s_new)

    return (xp.reshape(batch, t, d), xs.reshape(db, ts, d), jnp.stack(kp), jnp.stack(vp), jnp.stack(sp),
            jnp.stack(kss), jnp.stack(vss), jnp.stack(sss))
```

```python
import functools
import math

import jax
import jax.numpy as jnp
from jax import lax
from jax.experimental import pallas as pl
from jax.experimental.pallas import tpu as pltpu

F32 = jnp.float32
BF16 = jnp.bfloat16

H_A = 8
DH_A = 64
MOBA_BLOCK = 256
MOBA_TOPK = 3
ROPE_THETA = 500000.0
ROT_DIM = DH_A // 4
H_R = 4
DK_R = 128
DV_R = 256
RET_CHUNK = 128
RET_THETA = 10000.0
EPS = 1e-6
NEG = -1e30
BIG = 1e30
W_A = H_A * DH_A
W_RQK = H_R * DK_R
W_RV = H_R * DV_R
Q_SCALE = DH_A ** -0.5 * math.log2(math.e)

LANES = 128
BF16_ROWS = 16
VMEM_LIMIT = 56 * 1024 * 1024
PV_ROWS = DH_A + BF16_ROWS
KV_GROUP = 4


def _cparams(n_grid):
    return pltpu.CompilerParams(dimension_semantics=("arbitrary",) * n_grid,
                                vmem_limit_bytes=VMEM_LIMIT)


def _resident(shape):
    return pl.BlockSpec(shape, lambda *_: (0,) * len(shape), pipeline_mode=pl.Buffered(1))


def _rms_scale(y, g):
    return y * lax.rsqrt(jnp.mean(y * y, axis=-1, keepdims=True) + EPS) * g


def _dot(a, b):
    return jnp.dot(a, b, preferred_element_type=F32)


def _dot_nt(a, b):
    return lax.dot_general(a, b, (((1,), (1,)), ((), ())), preferred_element_type=F32)


def _in_proj_kernel(x_ref, g_ref, w_ref, ca_ref, s1_ref, s2_ref, cr_ref, sr_ref,
                    qa_ref, ka_ref, va_ref, kb_ref, vb_ref, qr_ref, kr_ref, vr_ref,
                    gr_ref, sga_ref, sgr_ref):
    d_model = x_ref.shape[1]
    h = _rms_scale(x_ref[...], g_ref[...]).astype(BF16)

    def proj(c0, n):
        return _dot(h, w_ref[:, c0:c0 + n])

    ca, s1, s2 = ca_ref[...], s1_ref[...], s2_ref[...]
    cr, sr = cr_ref[...], sr_ref[...]

    def rope_a(z):
        outs = []
        for c in range(z.shape[1] // LANES):
            zc = z[:, c * LANES:(c + 1) * LANES]
            outs.append(zc * ca + pltpu.roll(zc, ROT_DIM // 2, 1) * s1
                        + pltpu.roll(zc, LANES - ROT_DIM // 2, 1) * s2)
        return jnp.concatenate(outs, axis=1)

    def rope_r(z):
        outs = []
        for c in range(z.shape[1] // LANES):
            zc = z[:, c * LANES:(c + 1) * LANES]
            outs.append(zc * cr + pltpu.roll(zc, DK_R // 2, 1) * sr)
        return jnp.concatenate(outs, axis=1)

    c0 = 0
    qa_ref[...] = (rope_a(proj(c0, W_A)) * Q_SCALE).astype(qa_ref.dtype)
    c0 += W_A
    ka = rope_a(proj(c0, W_A))
    ka_ref[...] = ka
    kb_ref[...] = ka.astype(BF16)
    c0 += W_A
    va = proj(c0, W_A)
    va_ref[...] = va
    vb_ref[...] = va.astype(BF16)
    c0 += W_A
    qr_ref[...] = rope_r(proj(c0, W_RQK)).astype(qr_ref.dtype)
    c0 += W_RQK
    kr_ref[...] = (rope_r(proj(c0, W_RQK)) * (DK_R ** -0.5)).astype(kr_ref.dtype)
    c0 += W_RQK
    vr_ref[...] = proj(c0, W_RV).astype(vr_ref.dtype)
    c0 += W_RV
    gr = proj(c0, W_RV)
    gr_ref[...] = (gr * jax.nn.sigmoid(gr)).astype(gr_ref.dtype)
    c0 += W_RV
    sga_ref[...] = jax.nn.sigmoid(proj(c0, d_model)).astype(sga_ref.dtype)
    c0 += d_model
    sgr_ref[...] = jax.nn.sigmoid(proj(c0, d_model)).astype(sgr_ref.dtype)


def _in_proj(x, g, w, tabs, tm, n_tab_blocks):
    n, d = x.shape
    d_in = w.shape[1]
    rows = lambda width: pl.BlockSpec((tm, width), lambda i: (i, 0))
    tab = pl.BlockSpec((tm, LANES), lambda i: (i % n_tab_blocks, 0))
    widths = [W_A, W_A, W_A, W_A, W_A, W_RQK, W_RQK, W_RV, W_RV, d, d]
    dtypes = [BF16, F32, F32, BF16, BF16, BF16, BF16, BF16, BF16, BF16, BF16]
    return pl.pallas_call(
        _in_proj_kernel,
        grid=(n // tm,),
        in_specs=[rows(d), _resident((1, d)), _resident((d, d_in))] + [tab] * 5,
        out_specs=[rows(wd) for wd in widths],
        out_shape=[jax.ShapeDtypeStruct((n, wd), dt) for wd, dt in zip(widths, dtypes)],
        compiler_params=_cparams(1),
        name="in_proj",
    )(x, g, w, *tabs)


def _top3(s, axis):
    n = s.shape[axis]
    idx = lax.broadcasted_iota(jnp.int32, s.shape, axis).astype(F32)
    sel = jnp.zeros(s.shape, F32)
    for _ in range(MOBA_TOPK):
        m = jnp.max(s, axis=axis, keepdims=True)
        first = jnp.min(jnp.where(s == m, idx, float(n)), axis=axis, keepdims=True)
        pick = idx == first
        sel = jnp.where(pick, 1.0, sel)
        s = jnp.where(pick, -jnp.inf, s)
    return sel


def _moba_prompt_kernel(q_ref, k_ref, v_ref, o_ref, vt_ref, qt_ref, lg0_ref, lg1_ref, cmax_ref, csum_ref,
                        mu0_ref, mu1_ref, acc_ref):
    nb = k_ref.shape[0]
    nq = q_ref.shape[0]
    i = pl.program_id(2)

    @pl.when(i == 0)
    def _():
        cmax_ref[...] = jnp.zeros(cmax_ref.shape, F32)
        csum_ref[...] = jnp.zeros(csum_ref.shape, F32)
        pad = jnp.concatenate([jnp.ones((1, MOBA_BLOCK), F32),
                               jnp.zeros((PV_ROWS - DH_A - 1, MOBA_BLOCK), F32)], axis=0)
        for j in range(nb):
            vt = v_ref[j].astype(F32).T
            for hh in range(2):
                vt_ref[j, hh] = jnp.concatenate([vt[hh * DH_A:(hh + 1) * DH_A], pad], axis=0).astype(BF16)

    @pl.when(i < nb)
    def _():
        qt = q_ref[...].astype(F32).T
        drow = lax.broadcasted_iota(jnp.int32, qt.shape, 0)
        qt_ref[0] = jnp.where(drow < DH_A, qt, 0.0).astype(BF16)
        qt_ref[1] = jnp.where(drow >= DH_A, qt, 0.0).astype(BF16)

    for hh in range(2):
        acc_ref[hh] = jnp.zeros(acc_ref.shape[1:], F32)

    def step(lg_w, mu_w, lg_r, mu_r):
        def logits_block(j):
            kblk = k_ref[j]
            for hh in range(2):
                lg = _dot(kblk, qt_ref[hh])
                lg_w[hh, j] = lg
                cmax_ref[hh, pl.ds(j, 1), :] = jnp.max(lg, axis=0, keepdims=True)
                csum_ref[hh, pl.ds(j, 1), :] = jnp.sum(lg, axis=0, keepdims=True)

        def probs_group(t):
            out = []
            for hh in range(2):
                ps, vs = [], []
                for g in range(KV_GROUP):
                    jj = t * KV_GROUP + g
                    jc = jnp.minimum(jj, i - 1)
                    x = lg_r[hh, jc] - mu_r[hh, pl.ds(jj, 1), :]
                    ps.append(jnp.exp2(x.astype(BF16)))
                    vs.append(vt_ref[jc, hh])
                out.append((jnp.concatenate(vs, axis=1), jnp.concatenate(ps, axis=0)))
            return out

        def values_group(pv):
            for hh in range(2):
                acc_ref[hh] += _dot(*pv[hh])

        def fused(t, carry):
            pv = probs_group(t)
            for g in range(KV_GROUP):
                logits_block(t * KV_GROUP + g)
            values_group(pv)
            return carry

        def values_only(t, carry):
            values_group(probs_group(t))
            return carry

        trips = (i + KV_GROUP - 1) // KV_GROUP
        lax.fori_loop(0, jnp.where(i < nb, trips, 0), fused, 0)
        lax.fori_loop(0, jnp.where(i < nb, 0, trips), values_only, 0)

        @pl.when(i < nb)
        def _():
            kpos = lax.broadcasted_iota(jnp.int32, (MOBA_BLOCK, nq), 0)
            qpos = lax.broadcasted_iota(jnp.int32, (MOBA_BLOCK, nq), 1)
            blk = lax.broadcasted_iota(jnp.int32, (nb, nq), 0)
            kown = k_ref[i]
            for hh in range(2):
                lg = jnp.where(kpos <= qpos, _dot(kown, qt_ref[hh]), NEG)
                lg_w[hh, i] = lg
                m = jnp.max(lg, axis=0, keepdims=True)
                sel = _top3(jnp.where(blk < i, csum_ref[hh], -jnp.inf), 0)
                sel = (sel > 0.0) & (blk < i)
                m = jnp.maximum(m, jnp.max(jnp.where(sel, cmax_ref[hh], NEG), axis=0, keepdims=True))
                mu_w[hh] = jnp.where(sel | (blk == i), m, BIG)

    @pl.when((i & 1) == 0)
    def _():
        step(lg0_ref, mu0_ref, lg1_ref, mu1_ref)

    @pl.when((i & 1) == 1)
    def _():
        step(lg1_ref, mu1_ref, lg0_ref, mu0_ref)

    @pl.when(i >= 1)
    def _():
        outs = []
        for hh in range(2):
            acc = acc_ref[hh]
            outs.append(acc[:DH_A] / acc[DH_A:DH_A + 1])
        o_ref[...] = jnp.concatenate(outs, axis=0).T.astype(o_ref.dtype)


def _moba_prompt(qa, kb, vb, batch, t):
    nb = t // MOBA_BLOCK
    assert nb % KV_GROUP == 0
    npair = W_A // LANES
    q3 = qa.reshape(batch, t, W_A)
    k4 = kb.reshape(batch, nb, MOBA_BLOCK, W_A)
    v4 = vb.reshape(batch, nb, MOBA_BLOCK, W_A)
    kv_spec = pl.BlockSpec((None, nb, MOBA_BLOCK, LANES), lambda b, hp, i: (b, 0, 0, hp),
                           pipeline_mode=pl.Buffered(1))
    q_spec = pl.BlockSpec((None, MOBA_BLOCK, LANES), lambda b, hp, i: (b, jnp.minimum(i, nb - 1), hp))
    o_spec = pl.BlockSpec((None, MOBA_BLOCK, LANES), lambda b, hp, i: (b, jnp.maximum(i - 1, 0), hp))
    out = pl.pallas_call(
        _moba_prompt_kernel,
        grid=(batch, npair, nb + 1),
        in_specs=[q_spec, kv_spec, kv_spec],
        out_specs=o_spec,
        out_shape=jax.ShapeDtypeStruct((batch, t, W_A), BF16),
        scratch_shapes=[pltpu.VMEM((nb, 2, PV_ROWS, MOBA_BLOCK), BF16),
                        pltpu.VMEM((2, LANES, MOBA_BLOCK), BF16),
                        pltpu.VMEM((2, nb, MOBA_BLOCK, MOBA_BLOCK), F32),
                        pltpu.VMEM((2, nb, MOBA_BLOCK, MOBA_BLOCK), F32),
                        pltpu.VMEM((2, nb, MOBA_BLOCK), F32),
                        pltpu.VMEM((2, nb, MOBA_BLOCK), F32),
                        pltpu.VMEM((2, nb, MOBA_BLOCK), F32),
                        pltpu.VMEM((2, nb, MOBA_BLOCK), F32),
                        pltpu.VMEM((2, PV_ROWS, MOBA_BLOCK), F32)],
        compiler_params=_cparams(3),
        name="moba_prompt",
    )(q3, k4, v4)
    return out.reshape(batch * t, W_A)


def _lane_head(lane):
    return lax.shift_right_logical(lane, DH_A.bit_length() - 1)


def _moba_sample_kernel(pt_ref, q_ref, kn_ref, vn_ref, *rest, pps, nchunk, page):
    del pt_ref
    k_pages = rest[:pps]
    v_pages = rest[pps:2 * pps]
    o_ref = rest[2 * pps]
    lg_ref, bsum_ref, bmax_ref, muse_ref, acc_ref, lsum_ref, qrows_ref = rest[2 * pps + 1:]
    ts = q_ref.shape[0]
    nrow = H_A * ts
    n_pages = pps * nchunk
    ppb = MOBA_BLOCK // page
    nb = n_pages // ppb
    c = pl.program_id(1)
    lane = lax.broadcasted_iota(jnp.int32, (nrow, LANES), 1)

    @pl.when(c == 0)
    def _():
        qt = jnp.concatenate([q_ref[...]] * H_A, axis=0)
        r = lax.broadcasted_iota(jnp.int32, qt.shape, 0)
        hl = _lane_head(lax.broadcasted_iota(jnp.int32, qt.shape, 1))
        qrows_ref[...] = jnp.where((r >= hl * ts) & (r < hl * ts + ts), qt, 0.0).astype(BF16)
        bsum_ref[...] = jnp.full(bsum_ref.shape, -jnp.inf, F32)
        bmax_ref[...] = jnp.full(bmax_ref.shape, NEG, F32)
        acc_ref[...] = jnp.zeros(acc_ref.shape, F32)
        lsum_ref[...] = jnp.zeros(lsum_ref.shape, F32)

    @pl.when(c < nchunk)
    def _():
        qrb = qrows_ref[...]
        bsum, bmax = bsum_ref[...], bmax_ref[...]
        for tt in range(pps):
            ktp = k_pages[tt][...].reshape(W_A, page).astype(BF16)
            lg = _dot(qrb, ktp)
            lg_ref[c * pps + tt] = lg
            if tt % ppb == 0:
                s_acc, m_acc = lg, lg
            else:
                s_acc, m_acc = s_acc + lg, jnp.maximum(m_acc, lg)
            if tt % ppb == ppb - 1:
                here = lane == c * (pps // ppb) + tt // ppb
                bsum = jnp.where(here, jnp.sum(s_acc, axis=1, keepdims=True), bsum)
                bmax = jnp.where(here, jnp.max(m_acc, axis=1, keepdims=True), bmax)
        bsum_ref[...] = bsum
        bmax_ref[...] = bmax

    @pl.when(c == nchunk - 1)
    def _():
        sel = (_top3(bsum_ref[...], 1) > 0.0) & (lane < nb)
        kn = jnp.concatenate([kn_ref[...], jnp.zeros((page - ts, W_A), F32)], axis=0).astype(BF16)
        lo = _dot_nt(qrows_ref[...], kn)
        kpos = lax.broadcasted_iota(jnp.int32, lo.shape, 1)
        qidx = lax.broadcasted_iota(jnp.int32, lo.shape, 0) & (ts - 1)
        lo = jnp.where((kpos < ts) & (kpos <= qidx), lo, NEG)
        lg_ref[n_pages] = lo
        m = jnp.maximum(jnp.max(lo, axis=1, keepdims=True),
                        jnp.max(jnp.where(sel, bmax_ref[...], NEG), axis=1, keepdims=True))
        muse_ref[...] = jnp.where(sel | (lane == nb), m, BIG)

    def attend(g, j, vt_b):
        mj = jnp.sum(jnp.where(lane == j, muse_ref[...], 0.0), axis=1, keepdims=True)
        p = jnp.exp2((lg_ref[g] - mj).astype(BF16))
        lsum_ref[...] += p.astype(F32)
        acc_ref[...] += _dot_nt(p, vt_b)

    @pl.when(c >= nchunk)
    def _():
        for tt in range(pps):
            g = (c - nchunk) * pps + tt
            attend(g, (c - nchunk) * (pps // ppb) + tt // ppb, v_pages[tt][...].reshape(W_A, page).astype(BF16))

    @pl.when(c == 2 * nchunk - 1)
    def _():
        vn = jnp.concatenate([vn_ref[...], jnp.zeros((page - ts, W_A), F32)], axis=0)
        attend(n_pages, nb, vn.T.astype(BF16))
        acc = acc_ref[...] / jnp.sum(lsum_ref[...], axis=1, keepdims=True)
        a3 = acc.reshape(H_A, ts, W_A)
        hrow = lax.broadcasted_iota(jnp.int32, a3.shape, 0)
        hl = _lane_head(lax.broadcasted_iota(jnp.int32, a3.shape, 2))
        o_ref[...] = jnp.sum(jnp.where(hrow == hl, a3, 0.0), axis=0).astype(o_ref.dtype)


def _moba_sample(qa, kn, vn, cache_kt, cache_vt, page_table, layer, ts):
    db, n_pages = page_table.shape
    page = cache_kt.shape[4]
    assert page == LANES and n_pages * page // MOBA_BLOCK < LANES
    pps = min(32, n_pages)
    nchunk = n_pages // pps
    assert n_pages % pps == 0 and pps % (MOBA_BLOCK // page) == 0
    nrow = H_A * ts
    rows = pl.BlockSpec((ts, W_A), lambda b, c, pt: (b, 0))

    def kpage(tt):
        return pl.BlockSpec((None, None, H_A, DH_A, page),
                            lambda b, c, pt: (layer, pt[b, jnp.minimum(c, nchunk - 1) * pps + tt], 0, 0, 0))

    def vpage(tt):
        return pl.BlockSpec((None, None, H_A, DH_A, page),
                            lambda b, c, pt: (layer, pt[b, jnp.maximum(c - nchunk, 0) * pps + tt], 0, 0, 0))

    grid_spec = pltpu.PrefetchScalarGridSpec(
        num_scalar_prefetch=1,
        grid=(db, 2 * nchunk),
        in_specs=[rows, rows, rows] + [kpage(tt) for tt in range(pps)] + [vpage(tt) for tt in range(pps)],
        out_specs=rows,
        scratch_shapes=[pltpu.VMEM((n_pages + 1, nrow, page), F32),
                        pltpu.VMEM((nrow, LANES), F32),
                        pltpu.VMEM((nrow, LANES), F32),
                        pltpu.VMEM((nrow, LANES), F32),
                        pltpu.VMEM((nrow, W_A), F32),
                        pltpu.VMEM((nrow, page), F32),
                        pltpu.VMEM((nrow, W_A), BF16)],
    )
    return pl.pallas_call(
        functools.partial(_moba_sample_kernel, pps=pps, nchunk=nchunk, page=page),
        grid_spec=grid_spec,
        out_shape=jax.ShapeDtypeStruct((db * ts, W_A), F32),
        compiler_params=_cparams(2),
        name="moba_sample",
    )(page_table, qa, kn, vn, *([cache_kt] * pps), *([cache_vt] * pps))


def _ret_chunk(q, k, v, s, lg, c_true):
    n = q.shape[0]
    ii = lax.broadcasted_iota(jnp.int32, (n, n), 0)
    jj = lax.broadcasted_iota(jnp.int32, (n, n), 1)
    diff = (ii - jj).astype(F32)
    causal = diff >= 0
    dmat = jnp.where(causal, jnp.exp(jnp.where(causal, diff * lg, 0.0)), 0.0)
    ic = lax.broadcasted_iota(jnp.int32, (n, 1), 0).astype(F32)
    q_dec = jnp.exp((ic + 1.0) * lg)
    k_dec = jnp.exp((c_true - 1.0 - ic) * lg)
    c_dec = jnp.exp(jnp.full((1, 1), float(c_true), F32) * lg)
    att = _dot_nt(q, k) * dmat
    inner = _dot(att.astype(BF16), v)
    cross = _dot(q, s.astype(BF16)) * q_dec
    kd = (k.astype(F32) * k_dec).T.astype(BF16)
    s_new = s * c_dec + _dot(kd, v)
    return inner + cross, s_new


def _head_norm_gate(o, g):
    on = o * lax.rsqrt(jnp.mean(o * o, axis=-1, keepdims=True) + EPS)
    return g.astype(F32) * on.astype(F32)


def _ret_prompt_kernel(lg_ref, q_ref, k_ref, v_ref, g_ref, r_ref, s_ref):
    c = pl.program_id(1)

    @pl.when(c == 0)
    def _():
        s_ref[...] = jnp.zeros(s_ref.shape, F32)

    for h in range(H_R):
        qk = slice(h * DK_R, (h + 1) * DK_R)
        vv = slice(h * DV_R, (h + 1) * DV_R)
        o, s_new = _ret_chunk(q_ref[:, qk], k_ref[:, qk], v_ref[:, vv], s_ref[h], lg_ref[h], RET_CHUNK)
        s_ref[h] = s_new
        r_ref[:, vv] = _head_norm_gate(o, g_ref[:, vv]).astype(r_ref.dtype)


def _ret_prompt(log_g, qr, kr, vr, gr, batch, t):
    nc = t // RET_CHUNK
    qk = pl.BlockSpec((None, RET_CHUNK, W_RQK), lambda b, c, lg: (b, c, 0))
    vv = pl.BlockSpec((None, RET_CHUNK, W_RV), lambda b, c, lg: (b, c, 0))
    grid_spec = pltpu.PrefetchScalarGridSpec(
        num_scalar_prefetch=1,
        grid=(batch, nc),
        in_specs=[qk, qk, vv, vv],
        out_specs=[vv, pl.BlockSpec((None, H_R, DK_R, DV_R), lambda b, c, lg: (b, 0, 0, 0))],
    )
    r, s_fin = pl.pallas_call(
        _ret_prompt_kernel,
        grid_spec=grid_spec,
        out_shape=[jax.ShapeDtypeStruct((batch, t, W_RV), BF16),
                   jax.ShapeDtypeStruct((batch, H_R, DK_R, DV_R), F32)],
        compiler_params=_cparams(2),
        name="ret_prompt",
    )(log_g, qr.reshape(batch, t, W_RQK), kr.reshape(batch, t, W_RQK),
      vr.reshape(batch, t, W_RV), gr.reshape(batch, t, W_RV))
    return r.reshape(batch * t, W_RV), s_fin


def _ret_sample_kernel(lg_ref, q_ref, k_ref, v_ref, g_ref, s0_ref, r_ref, s_ref, *, ts, spb):
    h = pl.program_id(1)
    pad = lambda a: jnp.concatenate(
        [a, jnp.zeros((RET_CHUNK - ts, a.shape[1]), a.dtype)], axis=0).astype(BF16)
    outs = []
    for sidx in range(spb):
        sl = slice(sidx * ts, (sidx + 1) * ts)
        o, s_new = _ret_chunk(pad(q_ref[sl, :]), pad(k_ref[sl, :]), pad(v_ref[sl, :]),
                              s0_ref[sidx].astype(F32), lg_ref[h], ts)
        s_ref[sidx] = s_new.astype(s_ref.dtype)
        outs.append(_head_norm_gate(o[:ts], g_ref[sl, :]))
    r_ref[...] = jnp.concatenate(outs, axis=0).astype(r_ref.dtype)


def _ret_sample(log_g, qr, kr, vr, gr, state, layer, db, ts):
    spb = 2 if db % 2 == 0 else 1
    qk = pl.BlockSpec((spb * ts, DK_R), lambda b, h, lg: (b, h))
    vv = pl.BlockSpec((spb * ts, DV_R), lambda b, h, lg: (b, h))
    grid_spec = pltpu.PrefetchScalarGridSpec(
        num_scalar_prefetch=1,
        grid=(db // spb, H_R),
        in_specs=[qk, qk, vv, vv,
                  pl.BlockSpec((None, spb, None, DK_R, DV_R), lambda b, h, lg: (layer, b, h, 0, 0))],
        out_specs=[vv, pl.BlockSpec((spb, None, DK_R, DV_R), lambda b, h, lg: (b, h, 0, 0))],
    )
    return pl.pallas_call(
        functools.partial(_ret_sample_kernel, ts=ts, spb=spb),
        grid_spec=grid_spec,
        out_shape=[jax.ShapeDtypeStruct((db * ts, W_RV), BF16),
                   jax.ShapeDtypeStruct((db, H_R, DK_R, DV_R), state.dtype)],
        compiler_params=_cparams(2),
        name="ret_sample",
    )(log_g, qr, kr, vr, gr, state)


def _merge_kernel(x_ref, a_ref, r_ref, sga_ref, sgr_ref, wpa_ref, wpr_ref, wo_ref, g_ref, o_ref):
    merged = (sga_ref[...].astype(F32) * _dot(a_ref[...].astype(BF16), wpa_ref[...])
              + sgr_ref[...].astype(F32) * _dot(r_ref[...], wpr_ref[...]))
    y = _dot(merged.astype(BF16), wo_ref[...])
    o_ref[...] = x_ref[...] + _rms_scale(y, g_ref[...])


def _merge(x, a, r, sga, sgr, wpa, wpr, wo, g, tm):
    n, d = x.shape
    rows = lambda width: pl.BlockSpec((tm, width), lambda i: (i, 0))
    return pl.pallas_call(
        _merge_kernel,
        grid=(n // tm,),
        in_specs=[rows(d), rows(W_A), rows(W_RV), rows(d), rows(d),
                  _resident(wpa.shape), _resident(wpr.shape), _resident(wo.shape), _resident((1, d))],
        out_specs=rows(d),
        out_shape=jax.ShapeDtypeStruct((n, d), F32),
        compiler_params=_cparams(1),
        name="merge",
    )(x, a, r, sga, sgr, wpa, wpr, wo, g)


FF_CHUNK = 256


def _ffn_kernel(x_ref, gpre_ref, w1_ref, w2_ref, gpost_ref, o_ref):
    d_ff = w2_ref.shape[0]
    x = x_ref[...]
    h = _rms_scale(x, gpre_ref[...]).astype(BF16)
    y = jnp.zeros(x.shape, F32)
    for c0 in range(0, d_ff, FF_CHUNK):
        gt = _dot(h, w1_ref[:, c0:c0 + FF_CHUNK])
        up = _dot(h, w1_ref[:, d_ff + c0:d_ff + c0 + FF_CHUNK])
        act = (gt * jax.nn.sigmoid(gt) * up).astype(BF16)
        y = y + _dot(act, w2_ref[c0:c0 + FF_CHUNK, :])
    o_ref[...] = x + _rms_scale(y, gpost_ref[...])


def _ffn(x, gpre, w1, w2, gpost, tm):
    n, d = x.shape
    assert w2.shape[0] % FF_CHUNK == 0
    rows = pl.BlockSpec((tm, d), lambda i: (i, 0))
    return pl.pallas_call(
        _ffn_kernel,
        grid=(n // tm,),
        in_specs=[rows, _resident((1, d)), _resident(w1.shape), _resident(w2.shape), _resident((1, d))],
        out_specs=rows,
        out_shape=jax.ShapeDtypeStruct((n, d), F32),
        compiler_params=_cparams(1),
        name="ffn",
    )(x, gpre, w1, w2, gpost)


def _rope_tables(pos):
    p = pos.astype(F32)[:, None]
    n = pos.shape[0]
    fa = ROPE_THETA ** (-jnp.arange(0, ROT_DIM, 2, dtype=F32) / ROT_DIM)
    ang = p * fa[None, :]
    cos, sin = jnp.cos(ang), jnp.sin(ang)
    half = ROT_DIM // 2
    rest = DH_A - ROT_DIM
    z_h, z_r, o_r = jnp.zeros((n, half), F32), jnp.zeros((n, rest), F32), jnp.ones((n, rest), F32)
    two = lambda a: jnp.concatenate([a, a], axis=1)
    ca = two(jnp.concatenate([cos, cos, o_r], axis=1))
    s1 = two(jnp.concatenate([z_h, sin, z_r], axis=1))
    s2 = two(jnp.concatenate([-sin, z_h, z_r], axis=1))
    fr = 1.0 / (RET_THETA ** jnp.linspace(0.0, 1.0, DK_R // 2, dtype=F32))
    angr = p * fr[None, :]
    cr = two(jnp.cos(angr))
    sr = jnp.concatenate([-jnp.sin(angr), jnp.sin(angr)], axis=1)
    return ca, s1, s2, cr, sr


def _row_tile(n, cap):
    tm = min(n, cap)
    while n % tm:
        tm //= 2
    return tm


def kernel(x_prompt, x_sample, cache_k, cache_v, state_ret, page_table, g_mix_pre, w_in, w_proj_attn, w_proj_ret, w_out, g_mix_post, g_ffn_pre, w_ffn_in, w_ffn_out, g_ffn_post):
    batch, t, d = x_prompt.shape
    db, ts, _ = x_sample.shape
    depth = w_in.shape[0]
    page = cache_k.shape[2]
    n_pages = page_table.shape[1]
    past_len = n_pages * page
    assert cache_k.shape[3:] == (H_A, DH_A)
    assert t % MOBA_BLOCK == 0 and t % RET_CHUNK == 0 and past_len % MOBA_BLOCK == 0
    assert ts < RET_CHUNK and ts % 8 == 0 and ts & (ts - 1) == 0 and H_A * ts <= LANES
    assert MOBA_BLOCK % page == 0 and DH_A & (DH_A - 1) == 0

    tabs_p = _rope_tables(jnp.arange(t, dtype=jnp.int32))
    tabs_s = tuple(jnp.tile(tb, (db, 1)) for tb in _rope_tables(past_len + jnp.arange(ts, dtype=jnp.int32)))
    log_g = jnp.log(1.0 - 2.0 ** (-5.0 - jnp.arange(H_R, dtype=F32)))
    ckt = cache_k.transpose(0, 1, 3, 4, 2)
    cvt = cache_v.transpose(0, 1, 3, 4, 2)

    np_, ns_ = batch * t, db * ts
    tm_p = _row_tile(t, 512)
    tm_s = _row_tile(ns_, 512)
    xp = x_prompt.reshape(np_, d)
    xs = x_sample.reshape(ns_, d)
    row = lambda v: v.reshape(1, d)

    kp, vp, sp, kss, vss, sss = [], [], [], [], [], []
    for l in range(depth):
        w1 = w_in[l].astype(BF16)
        wpa, wpr, wo = w_proj_attn[l].astype(BF16), w_proj_ret[l].astype(BF16), w_out[l].astype(BF16)
        wf1, wf2 = w_ffn_in[l].astype(BF16), w_ffn_out[l].astype(BF16)

        qa, ka, va, kb, vb, qr, kr, vr, gr, sga, sgr = _in_proj(
            xp, row(g_mix_pre[l]), w1, tabs_p, tm_p, t // tm_p)
        a = _moba_prompt(qa, kb, vb, batch, t)
        r, s_fin = _ret_prompt(log_g, qr, kr, vr, gr, batch, t)
        xp = _merge(xp, a, r, sga, sgr, wpa, wpr, wo, row(g_mix_post[l]), tm_p)
        xp = _ffn(xp, row(g_ffn_pre[l]), wf1, wf2, row(g_ffn_post[l]), tm_p)
        kp.append(ka.reshape(batch, t, H_A, DH_A))
        vp.append(va.reshape(batch, t, H_A, DH_A))
        sp.append(s_fin)

        qa, ka, va, kb, vb, qr, kr, vr, gr, sga, sgr = _in_proj(
            xs, row(g_mix_pre[l]), w1, tabs_s, tm_s, ns_ // tm_s)
        f = lambda v: v.astype(F32)
        a = _moba_sample(f(qa), f(kb), f(vb), ckt, cvt, page_table, l, ts)
        r, s_new = _ret_sample(log_g, f(qr), f(kr), f(vr), f(gr), state_ret, l, db, ts)
        xs = _merge(xs, a, r, sga, sgr, wpa, wpr, wo, row(g_mix_post[l]), tm_s)
        xs = _ffn(xs, row(g_ffn_pre[l]), wf1, wf2, row(g_ffn_post[l]), tm_s)
        kss.append(ka.reshape(db, ts, H_A, DH_A))
        vss.append(va.reshape(db, ts, H_A, DH_A))
        sss.append(s_new)

    return (xp.reshape(batch, t, d), xs.reshape(db, ts, d), jnp.stack(kp), jnp.stack(vp), jnp.stack(sp),
            jnp.stack(kss), jnp.stack(vss), jnp.stack(sss))
```

```python
import functools
import math

import jax
import jax.numpy as jnp
from jax import lax
from jax.experimental import pallas as pl
from jax.experimental.pallas import tpu as pltpu

F32 = jnp.float32
BF16 = jnp.bfloat16

H_A = 8
DH_A = 64
MOBA_BLOCK = 256
MOBA_TOPK = 3
ROPE_THETA = 500000.0
ROT_DIM = DH_A // 4
H_R = 4
DK_R = 128
DV_R = 256
RET_CHUNK = 128
RET_THETA = 10000.0
EPS = 1e-6
NEG = -1e30
BIG = 1e30
W_A = H_A * DH_A
W_RQK = H_R * DK_R
W_RV = H_R * DV_R
Q_SCALE = DH_A ** -0.5 * math.log2(math.e)

LANES = 128
BF16_ROWS = 16
VMEM_LIMIT = 56 * 1024 * 1024
PV_ROWS = DH_A + BF16_ROWS
KV_GROUP = 4


def _cparams(n_grid):
    return pltpu.CompilerParams(dimension_semantics=("arbitrary",) * n_grid,
                                vmem_limit_bytes=VMEM_LIMIT)


def _resident(shape):
    return pl.BlockSpec(shape, lambda *_: (0,) * len(shape), pipeline_mode=pl.Buffered(1))


def _rms_scale(y, g):
    return y * lax.rsqrt(jnp.mean(y * y, axis=-1, keepdims=True) + EPS) * g


def _dot(a, b):
    return jnp.dot(a, b, preferred_element_type=F32)


def _dot_nt(a, b):
    return lax.dot_general(a, b, (((1,), (1,)), ((), ())), preferred_element_type=F32)


def _in_proj_kernel(x_ref, g_ref, w_ref, ca_ref, s1_ref, s2_ref, cr_ref, sr_ref, *rest, kv_transposed):
    if kv_transposed:
        rest = rest[2:]
    qa_ref, ka_ref, va_ref, kb_ref, vb_ref, qr_ref, kr_ref, vr_ref, gr_ref, sga_ref, sgr_ref = rest
    d_model = x_ref.shape[1]
    h = _rms_scale(x_ref[...], g_ref[...]).astype(BF16)

    def proj(c0, n):
        return _dot(h, w_ref[:, c0:c0 + n])

    ca, s1, s2 = ca_ref[...], s1_ref[...], s2_ref[...]
    cr, sr = cr_ref[...], sr_ref[...]

    def rope_a(z):
        outs = []
        for c in range(z.shape[1] // LANES):
            zc = z[:, c * LANES:(c + 1) * LANES]
            outs.append(zc * ca + pltpu.roll(zc, ROT_DIM // 2, 1) * s1
                        + pltpu.roll(zc, LANES - ROT_DIM // 2, 1) * s2)
        return jnp.concatenate(outs, axis=1)

    def rope_r(z):
        outs = []
        for c in range(z.shape[1] // LANES):
            zc = z[:, c * LANES:(c + 1) * LANES]
            outs.append(zc * cr + pltpu.roll(zc, DK_R // 2, 1) * sr)
        return jnp.concatenate(outs, axis=1)

    c0 = 0
    qa_ref[...] = (rope_a(proj(c0, W_A)) * Q_SCALE).astype(qa_ref.dtype)
    c0 += W_A
    ka = rope_a(proj(c0, W_A))
    ka_ref[...] = ka.T if kv_transposed else ka
    kb_ref[...] = ka.astype(BF16)
    c0 += W_A
    va = proj(c0, W_A)
    va_ref[...] = va.T if kv_transposed else va
    vb_ref[...] = va.astype(BF16)
    c0 += W_A
    qr_ref[...] = rope_r(proj(c0, W_RQK)).astype(qr_ref.dtype)
    c0 += W_RQK
    kr_ref[...] = (rope_r(proj(c0, W_RQK)) * (DK_R ** -0.5)).astype(kr_ref.dtype)
    c0 += W_RQK
    vr_ref[...] = proj(c0, W_RV).astype(vr_ref.dtype)
    c0 += W_RV
    gr = proj(c0, W_RV)
    gr_ref[...] = (gr * jax.nn.sigmoid(gr)).astype(gr_ref.dtype)
    c0 += W_RV
    sga_ref[...] = jax.nn.sigmoid(proj(c0, d_model)).astype(sga_ref.dtype)
    c0 += d_model
    sgr_ref[...] = jax.nn.sigmoid(proj(c0, d_model)).astype(sgr_ref.dtype)


def _in_proj(x, g, w, tabs, tm, n_tab_blocks, kv_stack=None, layer=None):
    n, d = x.shape
    d_in = w.shape[1]
    rows = lambda width: pl.BlockSpec((tm, width), lambda i: (i, 0))
    tab = pl.BlockSpec((tm, LANES), lambda i: (i % n_tab_blocks, 0))
    widths = [W_A, W_A, W_A, W_A, W_A, W_RQK, W_RQK, W_RV, W_RV, d, d]
    dtypes = [BF16, F32, F32, BF16, BF16, BF16, BF16, BF16, BF16, BF16, BF16]
    in_specs = [rows(d), _resident((1, d)), _resident((d, d_in))] + [tab] * 5
    out_specs = [rows(wd) for wd in widths]
    out_shape = [jax.ShapeDtypeStruct((n, wd), dt) for wd, dt in zip(widths, dtypes)]
    args = [x, g, w, *tabs]
    aliases = {}
    if kv_stack is not None:
        t_blocks = kv_stack[0].shape[3] // tm
        kv_spec = pl.BlockSpec((None, None, W_A, tm), lambda i: (layer, i // t_blocks, 0, i % t_blocks))
        for j, buf in enumerate(kv_stack):
            in_specs.append(pl.BlockSpec(memory_space=pl.ANY))
            aliases[len(args)] = 1 + j
            args.append(buf)
            out_specs[1 + j] = kv_spec
            out_shape[1 + j] = jax.ShapeDtypeStruct(buf.shape, buf.dtype)
    return pl.pallas_call(
        functools.partial(_in_proj_kernel, kv_transposed=kv_stack is not None),
        grid=(n // tm,),
        in_specs=in_specs,
        out_specs=out_specs,
        out_shape=out_shape,
        input_output_aliases=aliases,
        compiler_params=_cparams(1),
        name="in_proj",
    )(*args)


def _top3(s, axis):
    n = s.shape[axis]
    idx = lax.broadcasted_iota(jnp.int32, s.shape, axis).astype(F32)
    sel = jnp.zeros(s.shape, F32)
    for _ in range(MOBA_TOPK):
        m = jnp.max(s, axis=axis, keepdims=True)
        first = jnp.min(jnp.where(s == m, idx, float(n)), axis=axis, keepdims=True)
        pick = idx == first
        sel = jnp.where(pick, 1.0, sel)
        s = jnp.where(pick, -jnp.inf, s)
    return sel


def _moba_prompt_kernel(q_ref, k_ref, v_ref, o_ref, vt_ref, qt_ref, lg0_ref, lg1_ref, cmax_ref, csum_ref,
                        mu0_ref, mu1_ref, acc_ref):
    nb = k_ref.shape[0]
    nq = q_ref.shape[0]
    i = pl.program_id(2)

    @pl.when(i == 0)
    def _():
        cmax_ref[...] = jnp.zeros(cmax_ref.shape, F32)
        csum_ref[...] = jnp.zeros(csum_ref.shape, F32)
        pad = jnp.concatenate([jnp.ones((1, MOBA_BLOCK), F32),
                               jnp.zeros((PV_ROWS - DH_A - 1, MOBA_BLOCK), F32)], axis=0)
        for j in range(nb):
            vt = v_ref[j].astype(F32).T
            for hh in range(2):
                vt_ref[j, hh] = jnp.concatenate([vt[hh * DH_A:(hh + 1) * DH_A], pad], axis=0).astype(BF16)

    @pl.when(i < nb)
    def _():
        qt = q_ref[...].astype(F32).T
        drow = lax.broadcasted_iota(jnp.int32, qt.shape, 0)
        qt_ref[0] = jnp.where(drow < DH_A, qt, 0.0).astype(BF16)
        qt_ref[1] = jnp.where(drow >= DH_A, qt, 0.0).astype(BF16)

    for hh in range(2):
        acc_ref[hh] = jnp.zeros(acc_ref.shape[1:], F32)

    def step(lg_w, mu_w, lg_r, mu_r):
        def logits_block(j):
            kblk = k_ref[j]
            for hh in range(2):
                lg = _dot(kblk, qt_ref[hh])
                lg_w[hh, j] = lg
                cmax_ref[hh, pl.ds(j, 1), :] = jnp.max(lg, axis=0, keepdims=True)
                csum_ref[hh, pl.ds(j, 1), :] = jnp.sum(lg, axis=0, keepdims=True)

        def probs_group(t):
            out = []
            for hh in range(2):
                ps, vs = [], []
                for g in range(KV_GROUP):
                    jj = t * KV_GROUP + g
                    jc = jnp.minimum(jj, i - 1)
                    x = lg_r[hh, jc] - mu_r[hh, pl.ds(jj, 1), :]
                    ps.append(jnp.exp2(x.astype(BF16)))
                    vs.append(vt_ref[jc, hh])
                out.append((jnp.concatenate(vs, axis=1), jnp.concatenate(ps, axis=0)))
            return out

        def values_group(pv):
            for hh in range(2):
                acc_ref[hh] += _dot(*pv[hh])

        def fused(t, carry):
            pv = probs_group(t)
            for g in range(KV_GROUP):
                logits_block(t * KV_GROUP + g)
            values_group(pv)
            return carry

        def values_only(t, carry):
            values_group(probs_group(t))
            return carry

        trips = (i + KV_GROUP - 1) // KV_GROUP
        lax.fori_loop(0, jnp.where(i < nb, trips, 0), fused, 0)
        lax.fori_loop(0, jnp.where(i < nb, 0, trips), values_only, 0)

        @pl.when(i < nb)
        def _():
            kpos = lax.broadcasted_iota(jnp.int32, (MOBA_BLOCK, nq), 0)
            qpos = lax.broadcasted_iota(jnp.int32, (MOBA_BLOCK, nq), 1)
            blk = lax.broadcasted_iota(jnp.int32, (nb, nq), 0)
            kown = k_ref[i]
            for hh in range(2):
                lg = jnp.where(kpos <= qpos, _dot(kown, qt_ref[hh]), NEG)
                lg_w[hh, i] = lg
                m = jnp.max(lg, axis=0, keepdims=True)
                sel = _top3(jnp.where(blk < i, csum_ref[hh], -jnp.inf), 0)
                sel = (sel > 0.0) & (blk < i)
                m = jnp.maximum(m, jnp.max(jnp.where(sel, cmax_ref[hh], NEG), axis=0, keepdims=True))
                mu_w[hh] = jnp.where(sel | (blk == i), m, BIG)

    @pl.when((i & 1) == 0)
    def _():
        step(lg0_ref, mu0_ref, lg1_ref, mu1_ref)

    @pl.when((i & 1) == 1)
    def _():
        step(lg1_ref, mu1_ref, lg0_ref, mu0_ref)

    @pl.when(i >= 1)
    def _():
        outs = []
        for hh in range(2):
            acc = acc_ref[hh]
            outs.append(acc[:DH_A] / acc[DH_A:DH_A + 1])
        o_ref[...] = jnp.concatenate(outs, axis=0).T.astype(o_ref.dtype)


def _moba_prompt(qa, kb, vb, batch, t):
    nb = t // MOBA_BLOCK
    assert nb % KV_GROUP == 0
    npair = W_A // LANES
    q3 = qa.reshape(batch, t, W_A)
    k4 = kb.reshape(batch, nb, MOBA_BLOCK, W_A)
    v4 = vb.reshape(batch, nb, MOBA_BLOCK, W_A)
    kv_spec = pl.BlockSpec((None, nb, MOBA_BLOCK, LANES), lambda b, hp, i: (b, 0, 0, hp),
                           pipeline_mode=pl.Buffered(1))
    q_spec = pl.BlockSpec((None, MOBA_BLOCK, LANES), lambda b, hp, i: (b, jnp.minimum(i, nb - 1), hp))
    o_spec = pl.BlockSpec((None, MOBA_BLOCK, LANES), lambda b, hp, i: (b, jnp.maximum(i - 1, 0), hp))
    out = pl.pallas_call(
        _moba_prompt_kernel,
        grid=(batch, npair, nb + 1),
        in_specs=[q_spec, kv_spec, kv_spec],
        out_specs=o_spec,
        out_shape=jax.ShapeDtypeStruct((batch, t, W_A), BF16),
        scratch_shapes=[pltpu.VMEM((nb, 2, PV_ROWS, MOBA_BLOCK), BF16),
                        pltpu.VMEM((2, LANES, MOBA_BLOCK), BF16),
                        pltpu.VMEM((2, nb, MOBA_BLOCK, MOBA_BLOCK), F32),
                        pltpu.VMEM((2, nb, MOBA_BLOCK, MOBA_BLOCK), F32),
                        pltpu.VMEM((2, nb, MOBA_BLOCK), F32),
                        pltpu.VMEM((2, nb, MOBA_BLOCK), F32),
                        pltpu.VMEM((2, nb, MOBA_BLOCK), F32),
                        pltpu.VMEM((2, nb, MOBA_BLOCK), F32),
                        pltpu.VMEM((2, PV_ROWS, MOBA_BLOCK), F32)],
        compiler_params=_cparams(3),
        name="moba_prompt",
    )(q3, k4, v4)
    return out.reshape(batch * t, W_A)


def _lane_head(lane):
    return lax.shift_right_logical(lane, DH_A.bit_length() - 1)


def _moba_sample_kernel(pt_ref, q_ref, kn_ref, vn_ref, ck_ref, cv_ref, o_ref, ring_ref, sem_ref,
                        lg_ref, muse_ref, acc_ref, lsum_ref, qrows_ref, *, layer, cpp, n_pages):
    ts = q_ref.shape[0]
    nrow = H_A * ts
    page = ring_ref.shape[4]
    nslot = ring_ref.shape[0]
    nck = n_pages // cpp
    nchunks = 2 * nck
    ppb = MOBA_BLOCK // page
    nb = n_pages // ppb
    b = pl.program_id(0)
    lane = lax.broadcasted_iota(jnp.int32, (nrow, LANES), 1)

    def chunk_copies(bb, c):
        src, p0, slot = (ck_ref, c * cpp, c % nslot) if c < nck else (cv_ref, (c - nck) * cpp, c % nslot)
        return [pltpu.make_async_copy(src.at[layer, pt_ref[bb, p0 + tt]], ring_ref.at[slot, tt], sem_ref.at[slot])
                for tt in range(cpp)]

    def start_chunk(bb, c):
        for cp in chunk_copies(bb, c):
            cp.start()

    def wait_chunk(bb, c):
        for cp in chunk_copies(bb, c):
            cp.wait()

    def prefetch(c):
        ahead = c + nslot - 1
        if ahead < nchunks:
            start_chunk(b, ahead)
        else:
            @pl.when(b + 1 < pl.num_programs(0))
            def _():
                start_chunk(b + 1, ahead - nchunks)

    @pl.when(b == 0)
    def _():
        for c in range(nslot - 1):
            start_chunk(0, c)

    qt = jnp.concatenate([q_ref[...]] * H_A, axis=0)
    r = lax.broadcasted_iota(jnp.int32, qt.shape, 0)
    hl = _lane_head(lax.broadcasted_iota(jnp.int32, qt.shape, 1))
    qrows_ref[...] = jnp.where((r >= hl * ts) & (r < hl * ts + ts), qt, 0.0).astype(BF16)
    acc_ref[...] = jnp.zeros(acc_ref.shape, F32)
    lsum_ref[...] = jnp.zeros(lsum_ref.shape, F32)

    bsum = jnp.full((nrow, LANES), -jnp.inf, F32)
    bmax = jnp.full((nrow, LANES), NEG, F32)
    for c in range(nck):
        wait_chunk(b, c)
        prefetch(c)
        qrb = qrows_ref[...]
        for tt in range(cpp):
            ktp = ring_ref[c % nslot, tt].reshape(W_A, page).astype(BF16)
            lg = _dot(qrb, ktp)
            lg_ref[c * cpp + tt] = lg
            if tt % ppb == 0:
                s_acc, m_acc = lg, lg
            else:
                s_acc, m_acc = s_acc + lg, jnp.maximum(m_acc, lg)
            if tt % ppb == ppb - 1:
                here = lane == (c * cpp + tt) // ppb
                bsum = jnp.where(here, jnp.sum(s_acc, axis=1, keepdims=True), bsum)
                bmax = jnp.where(here, jnp.max(m_acc, axis=1, keepdims=True), bmax)

    sel = (_top3(bsum, 1) > 0.0) & (lane < nb)
    kn = jnp.concatenate([kn_ref[...], jnp.zeros((page - ts, W_A), F32)], axis=0).astype(BF16)
    lo = _dot_nt(qrows_ref[...], kn)
    kpos = lax.broadcasted_iota(jnp.int32, lo.shape, 1)
    qidx = lax.broadcasted_iota(jnp.int32, lo.shape, 0) & (ts - 1)
    lo = jnp.where((kpos < ts) & (kpos <= qidx), lo, NEG)
    lg_ref[n_pages] = lo
    m = jnp.maximum(jnp.max(lo, axis=1, keepdims=True),
                    jnp.max(jnp.where(sel, bmax, NEG), axis=1, keepdims=True))
    muse_ref[...] = jnp.where(sel | (lane == nb), m, BIG)

    def attend(g, j, vt_b):
        mj = jnp.sum(jnp.where(lane == j, muse_ref[...], 0.0), axis=1, keepdims=True)
        p = jnp.exp2((lg_ref[g] - mj).astype(BF16))
        lsum_ref[...] += p.astype(F32)
        acc_ref[...] += _dot_nt(p, vt_b)

    for c in range(nck, nchunks):
        wait_chunk(b, c)
        prefetch(c)
        for tt in range(cpp):
            g = (c - nck) * cpp + tt
            attend(g, g // ppb, ring_ref[c % nslot, tt].reshape(W_A, page).astype(BF16))

    vn = jnp.concatenate([vn_ref[...], jnp.zeros((page - ts, W_A), F32)], axis=0)
    attend(n_pages, nb, vn.T.astype(BF16))
    acc = acc_ref[...] / jnp.sum(lsum_ref[...], axis=1, keepdims=True)
    a3 = acc.reshape(H_A, ts, W_A)
    hrow = lax.broadcasted_iota(jnp.int32, a3.shape, 0)
    hl3 = _lane_head(lax.broadcasted_iota(jnp.int32, a3.shape, 2))
    o_ref[...] = jnp.sum(jnp.where(hrow == hl3, a3, 0.0), axis=0).astype(o_ref.dtype)


RING_SLOTS = 4
RING_CHUNK_PAGES = 16


def _moba_sample(qa, kn, vn, cache_kt, cache_vt, page_table, layer, ts):
    db, n_pages = page_table.shape
    page = cache_kt.shape[4]
    assert page == LANES and n_pages * page // MOBA_BLOCK < LANES
    cpp = min(RING_CHUNK_PAGES, n_pages)
    assert n_pages % cpp == 0 and cpp % (MOBA_BLOCK // page) == 0
    assert (2 * n_pages // cpp) % RING_SLOTS == 0
    nrow = H_A * ts
    rows = pl.BlockSpec((ts, W_A), lambda b, pt: (b, 0))
    hbm = pl.BlockSpec(memory_space=pl.ANY)
    grid_spec = pltpu.PrefetchScalarGridSpec(
        num_scalar_prefetch=1,
        grid=(db,),
        in_specs=[rows, rows, rows, hbm, hbm],
        out_specs=rows,
        scratch_shapes=[pltpu.VMEM((RING_SLOTS, cpp, H_A, DH_A, page), F32),
                        pltpu.SemaphoreType.DMA((RING_SLOTS,)),
                        pltpu.VMEM((n_pages + 1, nrow, page), F32),
                        pltpu.VMEM((nrow, LANES), F32),
                        pltpu.VMEM((nrow, W_A), F32),
                        pltpu.VMEM((nrow, page), F32),
                        pltpu.VMEM((nrow, W_A), BF16)],
    )
    return pl.pallas_call(
        functools.partial(_moba_sample_kernel, layer=layer, cpp=cpp, n_pages=n_pages),
        grid_spec=grid_spec,
        out_shape=jax.ShapeDtypeStruct((db * ts, W_A), F32),
        compiler_params=_cparams(1),
        name="moba_sample",
    )(page_table, qa, kn, vn, cache_kt, cache_vt)


def _ret_chunk(q, k, v, s, lg, c_true):
    n = q.shape[0]
    ii = lax.broadcasted_iota(jnp.int32, (n, n), 0)
    jj = lax.broadcasted_iota(jnp.int32, (n, n), 1)
    diff = (ii - jj).astype(F32)
    causal = diff >= 0
    dmat = jnp.where(causal, jnp.exp(jnp.where(causal, diff * lg, 0.0)), 0.0)
    ic = lax.broadcasted_iota(jnp.int32, (n, 1), 0).astype(F32)
    q_dec = jnp.exp((ic + 1.0) * lg)
    k_dec = jnp.exp((c_true - 1.0 - ic) * lg)
    c_dec = jnp.exp(jnp.full((1, 1), float(c_true), F32) * lg)
    att = _dot_nt(q, k) * dmat
    inner = _dot(att.astype(BF16), v)
    cross = _dot(q, s.astype(BF16)) * q_dec
    kd = (k.astype(F32) * k_dec).T.astype(BF16)
    s_new = s * c_dec + _dot(kd, v)
    return inner + cross, s_new


def _head_norm_gate(o, g):
    on = o * lax.rsqrt(jnp.mean(o * o, axis=-1, keepdims=True) + EPS)
    return g.astype(F32) * on.astype(F32)


def _ret_prompt_kernel(lg_ref, q_ref, k_ref, v_ref, g_ref, r_ref, s_ref):
    c = pl.program_id(1)

    @pl.when(c == 0)
    def _():
        s_ref[...] = jnp.zeros(s_ref.shape, F32)

    for cc in range(q_ref.shape[0] // RET_CHUNK):
        rows = slice(cc * RET_CHUNK, (cc + 1) * RET_CHUNK)
        for h in range(H_R):
            qk = slice(h * DK_R, (h + 1) * DK_R)
            vv = slice(h * DV_R, (h + 1) * DV_R)
            o, s_new = _ret_chunk(q_ref[rows, qk], k_ref[rows, qk], v_ref[rows, vv], s_ref[h], lg_ref[h],
                                  RET_CHUNK)
            s_ref[h] = s_new
            r_ref[rows, vv] = _head_norm_gate(o, g_ref[rows, vv]).astype(r_ref.dtype)


def _ret_prompt(log_g, qr, kr, vr, gr, batch, t):
    rows = 2 * RET_CHUNK if t % (2 * RET_CHUNK) == 0 else RET_CHUNK
    nc = t // rows
    qk = pl.BlockSpec((None, rows, W_RQK), lambda b, c, lg: (b, c, 0))
    vv = pl.BlockSpec((None, rows, W_RV), lambda b, c, lg: (b, c, 0))
    grid_spec = pltpu.PrefetchScalarGridSpec(
        num_scalar_prefetch=1,
        grid=(batch, nc),
        in_specs=[qk, qk, vv, vv],
        out_specs=[vv, pl.BlockSpec((None, H_R, DK_R, DV_R), lambda b, c, lg: (b, 0, 0, 0))],
    )
    r, s_fin = pl.pallas_call(
        _ret_prompt_kernel,
        grid_spec=grid_spec,
        out_shape=[jax.ShapeDtypeStruct((batch, t, W_RV), BF16),
                   jax.ShapeDtypeStruct((batch, H_R, DK_R, DV_R), F32)],
        compiler_params=_cparams(2),
        name="ret_prompt",
    )(log_g, qr.reshape(batch, t, W_RQK), kr.reshape(batch, t, W_RQK),
      vr.reshape(batch, t, W_RV), gr.reshape(batch, t, W_RV))
    return r.reshape(batch * t, W_RV), s_fin


def _ret_sample_kernel(lg_ref, q_ref, k_ref, v_ref, g_ref, s0_ref, r_ref, s_ref, *, ts, spb):
    h = pl.program_id(1)
    pad = lambda a: jnp.concatenate(
        [a, jnp.zeros((RET_CHUNK - ts, a.shape[1]), a.dtype)], axis=0).astype(BF16)
    outs = []
    for sidx in range(spb):
        sl = slice(sidx * ts, (sidx + 1) * ts)
        o, s_new = _ret_chunk(pad(q_ref[sl, :]), pad(k_ref[sl, :]), pad(v_ref[sl, :]),
                              s0_ref[sidx].astype(F32), lg_ref[h], ts)
        s_ref[sidx] = s_new.astype(s_ref.dtype)
        outs.append(_head_norm_gate(o[:ts], g_ref[sl, :]))
    r_ref[...] = jnp.concatenate(outs, axis=0).astype(r_ref.dtype)


def _ret_sample(log_g, qr, kr, vr, gr, state, layer, db, ts):
    spb = 2 if db % 2 == 0 else 1
    qk = pl.BlockSpec((spb * ts, DK_R), lambda b, h, lg: (b, h))
    vv = pl.BlockSpec((spb * ts, DV_R), lambda b, h, lg: (b, h))
    grid_spec = pltpu.PrefetchScalarGridSpec(
        num_scalar_prefetch=1,
        grid=(db // spb, H_R),
        in_specs=[qk, qk, vv, vv,
                  pl.BlockSpec((None, spb, None, DK_R, DV_R), lambda b, h, lg: (layer, b, h, 0, 0))],
        out_specs=[vv, pl.BlockSpec((spb, None, DK_R, DV_R), lambda b, h, lg: (b, h, 0, 0))],
    )
    return pl.pallas_call(
        functools.partial(_ret_sample_kernel, ts=ts, spb=spb),
        grid_spec=grid_spec,
        out_shape=[jax.ShapeDtypeStruct((db * ts, W_RV), BF16),
                   jax.ShapeDtypeStruct((db, H_R, DK_R, DV_R), state.dtype)],
        compiler_params=_cparams(2),
        name="ret_sample",
    )(log_g, qr, kr, vr, gr, state)


def _merge_kernel(x_ref, a_ref, r_ref, sga_ref, sgr_ref, wpa_ref, wpr_ref, wo_ref, g_ref, o_ref):
    merged = (sga_ref[...].astype(F32) * _dot(a_ref[...].astype(BF16), wpa_ref[...])
              + sgr_ref[...].astype(F32) * _dot(r_ref[...], wpr_ref[...]))
    y = _dot(merged.astype(BF16), wo_ref[...])
    o_ref[...] = x_ref[...] + _rms_scale(y, g_ref[...])


def _merge(x, a, r, sga, sgr, wpa, wpr, wo, g, tm):
    n, d = x.shape
    rows = lambda width: pl.BlockSpec((tm, width), lambda i: (i, 0))
    return pl.pallas_call(
        _merge_kernel,
        grid=(n // tm,),
        in_specs=[rows(d), rows(W_A), rows(W_RV), rows(d), rows(d),
                  _resident(wpa.shape), _resident(wpr.shape), _resident(wo.shape), _resident((1, d))],
        out_specs=rows(d),
        out_shape=jax.ShapeDtypeStruct((n, d), F32),
        compiler_params=_cparams(1),
        name="merge",
    )(x, a, r, sga, sgr, wpa, wpr, wo, g)


FF_CHUNK = 256


def _ffn_kernel(x_ref, gpre_ref, w1_ref, w2_ref, gpost_ref, o_ref):
    d_ff = w2_ref.shape[0]
    x = x_ref[...]
    h = _rms_scale(x, gpre_ref[...]).astype(BF16)
    y = jnp.zeros(x.shape, F32)
    for c0 in range(0, d_ff, FF_CHUNK):
        gt = _dot(h, w1_ref[:, c0:c0 + FF_CHUNK])
        up = _dot(h, w1_ref[:, d_ff + c0:d_ff + c0 + FF_CHUNK])
        act = (gt * jax.nn.sigmoid(gt) * up).astype(BF16)
        y = y + _dot(act, w2_ref[c0:c0 + FF_CHUNK, :])
    o_ref[...] = x + _rms_scale(y, gpost_ref[...])


def _ffn(x, gpre, w1, w2, gpost, tm):
    n, d = x.shape
    assert w2.shape[0] % FF_CHUNK == 0
    rows = pl.BlockSpec((tm, d), lambda i: (i, 0))
    return pl.pallas_call(
        _ffn_kernel,
        grid=(n // tm,),
        in_specs=[rows, _resident((1, d)), _resident(w1.shape), _resident(w2.shape), _resident((1, d))],
        out_specs=rows,
        out_shape=jax.ShapeDtypeStruct((n, d), F32),
        compiler_params=_cparams(1),
        name="ffn",
    )(x, gpre, w1, w2, gpost)


def _rope_tables(pos):
    p = pos.astype(F32)[:, None]
    n = pos.shape[0]
    fa = ROPE_THETA ** (-jnp.arange(0, ROT_DIM, 2, dtype=F32) / ROT_DIM)
    ang = p * fa[None, :]
    cos, sin = jnp.cos(ang), jnp.sin(ang)
    half = ROT_DIM // 2
    rest = DH_A - ROT_DIM
    z_h, z_r, o_r = jnp.zeros((n, half), F32), jnp.zeros((n, rest), F32), jnp.ones((n, rest), F32)
    two = lambda a: jnp.concatenate([a, a], axis=1)
    ca = two(jnp.concatenate([cos, cos, o_r], axis=1))
    s1 = two(jnp.concatenate([z_h, sin, z_r], axis=1))
    s2 = two(jnp.concatenate([-sin, z_h, z_r], axis=1))
    fr = 1.0 / (RET_THETA ** jnp.linspace(0.0, 1.0, DK_R // 2, dtype=F32))
    angr = p * fr[None, :]
    cr = two(jnp.cos(angr))
    sr = jnp.concatenate([-jnp.sin(angr), jnp.sin(angr)], axis=1)
    return ca, s1, s2, cr, sr


def _row_tile(n, cap):
    tm = min(n, cap)
    while n % tm:
        tm //= 2
    return tm


def kernel(x_prompt, x_sample, cache_k, cache_v, state_ret, page_table, g_mix_pre, w_in, w_proj_attn, w_proj_ret, w_out, g_mix_post, g_ffn_pre, w_ffn_in, w_ffn_out, g_ffn_post):
    batch, t, d = x_prompt.shape
    db, ts, _ = x_sample.shape
    depth = w_in.shape[0]
    page = cache_k.shape[2]
    n_pages = page_table.shape[1]
    past_len = n_pages * page
    assert cache_k.shape[3:] == (H_A, DH_A)
    assert t % MOBA_BLOCK == 0 and t % RET_CHUNK == 0 and past_len % MOBA_BLOCK == 0
    assert ts < RET_CHUNK and ts % 8 == 0 and ts & (ts - 1) == 0 and H_A * ts <= LANES
    assert MOBA_BLOCK % page == 0 and DH_A & (DH_A - 1) == 0

    tabs_p = _rope_tables(jnp.arange(t, dtype=jnp.int32))
    tabs_s = tuple(jnp.tile(tb, (db, 1)) for tb in _rope_tables(past_len + jnp.arange(ts, dtype=jnp.int32)))
    log_g = jnp.log(1.0 - 2.0 ** (-5.0 - jnp.arange(H_R, dtype=F32)))
    ckt = cache_k.transpose(0, 1, 3, 4, 2)
    cvt = cache_v.transpose(0, 1, 3, 4, 2)

    np_, ns_ = batch * t, db * ts
    tm_p = _row_tile(t, 512)
    tm_s = _row_tile(ns_, 512)
    xp = x_prompt.reshape(np_, d)
    xs = x_sample.reshape(ns_, d)
    row = lambda v: v.reshape(1, d)

    kt = jnp.zeros((depth, batch, W_A, t), F32)
    vt = jnp.zeros((depth, batch, W_A, t), F32)
    sp, kss, vss, sss = [], [], [], []
    for l in range(depth):
        w1 = w_in[l].astype(BF16)
        wpa, wpr, wo = w_proj_attn[l].astype(BF16), w_proj_ret[l].astype(BF16), w_out[l].astype(BF16)
        wf1, wf2 = w_ffn_in[l].astype(BF16), w_ffn_out[l].astype(BF16)

        qa, kt, vt, kb, vb, qr, kr, vr, gr, sga, sgr = _in_proj(
            xp, row(g_mix_pre[l]), w1, tabs_p, tm_p, t // tm_p, kv_stack=(kt, vt), layer=l)
        a = _moba_prompt(qa, kb, vb, batch, t)
        r, s_fin = _ret_prompt(log_g, qr, kr, vr, gr, batch, t)
        xp = _merge(xp, a, r, sga, sgr, wpa, wpr, wo, row(g_mix_post[l]), tm_p)
        xp = _ffn(xp, row(g_ffn_pre[l]), wf1, wf2, row(g_ffn_post[l]), tm_p)
        sp.append(s_fin)

        qa, ka, va, kb, vb, qr, kr, vr, gr, sga, sgr = _in_proj(
            xs, row(g_mix_pre[l]), w1, tabs_s, tm_s, ns_ // tm_s)
        f = lambda v: v.astype(F32)
        a = _moba_sample(f(qa), f(kb), f(vb), ckt, cvt, page_table, l, ts)
        r, s_new = _ret_sample(log_g, f(qr), f(kr), f(vr), f(gr), state_ret, l, db, ts)
        xs = _merge(xs, a, r, sga, sgr, wpa, wpr, wo, row(g_mix_post[l]), tm_s)
        xs = _ffn(xs, row(g_ffn_pre[l]), wf1, wf2, row(g_ffn_post[l]), tm_s)
        kss.append(ka.reshape(db, ts, H_A, DH_A))
        vss.append(va.reshape(db, ts, H_A, DH_A))
        sss.append(s_new)

    heads_last = lambda a: a.reshape(depth, batch, H_A, DH_A, t).transpose(0, 1, 4, 2, 3)
    return (xp.reshape(batch, t, d), xs.reshape(db, ts, d), heads_last(kt), heads_last(vt), jnp.stack(sp),
            jnp.stack(kss), jnp.stack(vss), jnp.stack(sss))
```

```python
import functools
import math

import jax
import jax.numpy as jnp
from jax import lax
from jax.experimental import pallas as pl
from jax.experimental.pallas import tpu as pltpu

F32 = jnp.float32
BF16 = jnp.bfloat16

H_A = 8
DH_A = 64
MOBA_BLOCK = 256
MOBA_TOPK = 3
ROPE_THETA = 500000.0
ROT_DIM = DH_A // 4
H_R = 4
DK_R = 128
DV_R = 256
RET_CHUNK = 128
RET_THETA = 10000.0
EPS = 1e-6
NEG = -1e30
BIG = 1e30
W_A = H_A * DH_A
W_RQK = H_R * DK_R
W_RV = H_R * DV_R
Q_SCALE = DH_A ** -0.5 * math.log2(math.e)

LANES = 128
BF16_ROWS = 16
VMEM_LIMIT = 56 * 1024 * 1024
PV_ROWS = DH_A + BF16_ROWS
KV_GROUP = 4


def _cparams(n_grid):
    return pltpu.CompilerParams(dimension_semantics=("arbitrary",) * n_grid,
                                vmem_limit_bytes=VMEM_LIMIT)


def _layer_slab(stacked, layer):
    return pl.BlockSpec((None,) + stacked.shape[1:], lambda *_: (layer, 0, 0), pipeline_mode=pl.Buffered(1))


def _rms_scale(y, g):
    return y * lax.rsqrt(jnp.mean(y * y, axis=-1, keepdims=True) + EPS) * g


def _dot(a, b):
    return jnp.dot(a, b, preferred_element_type=F32)


def _dot_nt(a, b):
    return lax.dot_general(a, b, (((1,), (1,)), ((), ())), preferred_element_type=F32)


def _in_proj_kernel(x_ref, g_ref, w_ref, ca_ref, s1_ref, s2_ref, cr_ref, sr_ref, *rest, kv_transposed):
    if kv_transposed:
        rest = rest[2:]
    qa_ref, ka_ref, va_ref, kb_ref, vb_ref, qr_ref, kr_ref, vr_ref, gr_ref, sga_ref, sgr_ref = rest
    d_model = x_ref.shape[1]
    h = _rms_scale(x_ref[...], g_ref[...]).astype(BF16)

    def proj(c0, n):
        return _dot(h, w_ref[:, c0:c0 + n])

    ca, s1, s2 = ca_ref[...], s1_ref[...], s2_ref[...]
    cr, sr = cr_ref[...], sr_ref[...]

    def rope_a(z):
        outs = []
        for c in range(z.shape[1] // LANES):
            zc = z[:, c * LANES:(c + 1) * LANES]
            outs.append(zc * ca + pltpu.roll(zc, ROT_DIM // 2, 1) * s1
                        + pltpu.roll(zc, LANES - ROT_DIM // 2, 1) * s2)
        return jnp.concatenate(outs, axis=1)

    def rope_r(z):
        outs = []
        for c in range(z.shape[1] // LANES):
            zc = z[:, c * LANES:(c + 1) * LANES]
            outs.append(zc * cr + pltpu.roll(zc, DK_R // 2, 1) * sr)
        return jnp.concatenate(outs, axis=1)

    c0 = 0
    qa_ref[...] = (rope_a(proj(c0, W_A)) * Q_SCALE).astype(qa_ref.dtype)
    c0 += W_A
    ka = rope_a(proj(c0, W_A))
    ka_ref[...] = ka.T if kv_transposed else ka
    kb_ref[...] = ka.astype(BF16)
    c0 += W_A
    va = proj(c0, W_A)
    va_ref[...] = va.T if kv_transposed else va
    vb_ref[...] = va.astype(BF16)
    c0 += W_A
    qr_ref[...] = rope_r(proj(c0, W_RQK)).astype(qr_ref.dtype)
    c0 += W_RQK
    kr_ref[...] = (rope_r(proj(c0, W_RQK)) * (DK_R ** -0.5)).astype(kr_ref.dtype)
    c0 += W_RQK
    vr_ref[...] = proj(c0, W_RV).astype(vr_ref.dtype)
    c0 += W_RV
    gr = proj(c0, W_RV)
    gr_ref[...] = (gr * jax.nn.sigmoid(gr)).astype(gr_ref.dtype)
    c0 += W_RV
    sga_ref[...] = jax.nn.sigmoid(proj(c0, d_model)).astype(sga_ref.dtype)
    c0 += d_model
    sgr_ref[...] = jax.nn.sigmoid(proj(c0, d_model)).astype(sgr_ref.dtype)


def _in_proj(x, g, w, tabs, tm, n_tab_blocks, layer, kv_stack=None):
    n, d = x.shape
    rows = lambda width: pl.BlockSpec((tm, width), lambda i: (i, 0))
    tab = pl.BlockSpec((tm, LANES), lambda i: (i % n_tab_blocks, 0))
    widths = [W_A, W_A, W_A, W_A, W_A, W_RQK, W_RQK, W_RV, W_RV, d, d]
    dtypes = [BF16, F32, F32, BF16, BF16, BF16, BF16, BF16, BF16, BF16, BF16]
    in_specs = [rows(d), _layer_slab(g, layer), _layer_slab(w, layer)] + [tab] * 5
    out_specs = [rows(wd) for wd in widths]
    out_shape = [jax.ShapeDtypeStruct((n, wd), dt) for wd, dt in zip(widths, dtypes)]
    args = [x, g, w, *tabs]
    aliases = {}
    if kv_stack is not None:
        t_blocks = kv_stack[0].shape[3] // tm
        kv_spec = pl.BlockSpec((None, None, W_A, tm), lambda i: (layer, i // t_blocks, 0, i % t_blocks))
        for j, buf in enumerate(kv_stack):
            in_specs.append(pl.BlockSpec(memory_space=pl.ANY))
            aliases[len(args)] = 1 + j
            args.append(buf)
            out_specs[1 + j] = kv_spec
            out_shape[1 + j] = jax.ShapeDtypeStruct(buf.shape, buf.dtype)
    return pl.pallas_call(
        functools.partial(_in_proj_kernel, kv_transposed=kv_stack is not None),
        grid=(n // tm,),
        in_specs=in_specs,
        out_specs=out_specs,
        out_shape=out_shape,
        input_output_aliases=aliases,
        compiler_params=_cparams(1),
        name="in_proj",
    )(*args)


def _top3(s, axis):
    n = s.shape[axis]
    idx = lax.broadcasted_iota(jnp.int32, s.shape, axis).astype(F32)
    sel = jnp.zeros(s.shape, F32)
    for _ in range(MOBA_TOPK):
        m = jnp.max(s, axis=axis, keepdims=True)
        first = jnp.min(jnp.where(s == m, idx, float(n)), axis=axis, keepdims=True)
        pick = idx == first
        sel = jnp.where(pick, 1.0, sel)
        s = jnp.where(pick, -jnp.inf, s)
    return sel


def _moba_prompt_kernel(q_ref, k_ref, v_ref, o_ref, vt_ref, qt_ref, lg0_ref, lg1_ref, cmax_ref, csum_ref,
                        mu0_ref, mu1_ref, acc_ref):
    nb = k_ref.shape[0]
    nq = q_ref.shape[0]
    i = pl.program_id(2)

    @pl.when(i == 0)
    def _():
        cmax_ref[...] = jnp.zeros(cmax_ref.shape, F32)
        csum_ref[...] = jnp.zeros(csum_ref.shape, F32)
        pad = jnp.concatenate([jnp.ones((1, MOBA_BLOCK), F32),
                               jnp.zeros((PV_ROWS - DH_A - 1, MOBA_BLOCK), F32)], axis=0)
        for j in range(nb):
            vt = v_ref[j].astype(F32).T
            for hh in range(2):
                vt_ref[j, hh] = jnp.concatenate([vt[hh * DH_A:(hh + 1) * DH_A], pad], axis=0).astype(BF16)

    @pl.when(i < nb)
    def _():
        qt = q_ref[...].astype(F32).T
        drow = lax.broadcasted_iota(jnp.int32, qt.shape, 0)
        qt_ref[0] = jnp.where(drow < DH_A, qt, 0.0).astype(BF16)
        qt_ref[1] = jnp.where(drow >= DH_A, qt, 0.0).astype(BF16)

    for hh in range(2):
        acc_ref[hh] = jnp.zeros(acc_ref.shape[1:], F32)

    def step(lg_w, mu_w, lg_r, mu_r):
        def logits_block(j):
            kblk = k_ref[j]
            for hh in range(2):
                lg = _dot(kblk, qt_ref[hh])
                lg_w[hh, j] = lg
                cmax_ref[hh, pl.ds(j, 1), :] = jnp.max(lg, axis=0, keepdims=True)
                csum_ref[hh, pl.ds(j, 1), :] = jnp.sum(lg, axis=0, keepdims=True)

        def probs_group(t):
            out = []
            for hh in range(2):
                ps, vs = [], []
                for g in range(KV_GROUP):
                    jj = t * KV_GROUP + g
                    jc = jnp.minimum(jj, i - 1)
                    x = lg_r[hh, jc] - mu_r[hh, pl.ds(jj, 1), :]
                    ps.append(jnp.exp2(x.astype(BF16)))
                    vs.append(vt_ref[jc, hh])
                out.append((jnp.concatenate(vs, axis=1), jnp.concatenate(ps, axis=0)))
            return out

        def values_group(pv):
            for hh in range(2):
                acc_ref[hh] += _dot(*pv[hh])

        def fused(t, carry):
            pv = probs_group(t)
            for g in range(KV_GROUP):
                logits_block(t * KV_GROUP + g)
            values_group(pv)
            return carry

        def values_only(t, carry):
            values_group(probs_group(t))
            return carry

        trips = (i + KV_GROUP - 1) // KV_GROUP
        lax.fori_loop(0, jnp.where(i < nb, trips, 0), fused, 0)
        lax.fori_loop(0, jnp.where(i < nb, 0, trips), values_only, 0)

        @pl.when(i < nb)
        def _():
            kpos = lax.broadcasted_iota(jnp.int32, (MOBA_BLOCK, nq), 0)
            qpos = lax.broadcasted_iota(jnp.int32, (MOBA_BLOCK, nq), 1)
            blk = lax.broadcasted_iota(jnp.int32, (nb, nq), 0)
            kown = k_ref[i]
            for hh in range(2):
                lg = jnp.where(kpos <= qpos, _dot(kown, qt_ref[hh]), NEG)
                lg_w[hh, i] = lg
                m = jnp.max(lg, axis=0, keepdims=True)
                sel = _top3(jnp.where(blk < i, csum_ref[hh], -jnp.inf), 0)
                sel = (sel > 0.0) & (blk < i)
                m = jnp.maximum(m, jnp.max(jnp.where(sel, cmax_ref[hh], NEG), axis=0, keepdims=True))
                mu_w[hh] = jnp.where(sel | (blk == i), m, BIG)

    @pl.when((i & 1) == 0)
    def _():
        step(lg0_ref, mu0_ref, lg1_ref, mu1_ref)

    @pl.when((i & 1) == 1)
    def _():
        step(lg1_ref, mu1_ref, lg0_ref, mu0_ref)

    @pl.when(i >= 1)
    def _():
        outs = []
        for hh in range(2):
            acc = acc_ref[hh]
            outs.append(acc[:DH_A] / acc[DH_A:DH_A + 1])
        o_ref[...] = jnp.concatenate(outs, axis=0).T.astype(o_ref.dtype)


def _moba_prompt(qa, kb, vb, batch, t):
    nb = t // MOBA_BLOCK
    assert nb % KV_GROUP == 0
    npair = W_A // LANES
    q3 = qa.reshape(batch, t, W_A)
    k4 = kb.reshape(batch, nb, MOBA_BLOCK, W_A)
    v4 = vb.reshape(batch, nb, MOBA_BLOCK, W_A)
    kv_spec = pl.BlockSpec((None, nb, MOBA_BLOCK, LANES), lambda b, hp, i: (b, 0, 0, hp),
                           pipeline_mode=pl.Buffered(1))
    q_spec = pl.BlockSpec((None, MOBA_BLOCK, LANES), lambda b, hp, i: (b, jnp.minimum(i, nb - 1), hp))
    o_spec = pl.BlockSpec((None, MOBA_BLOCK, LANES), lambda b, hp, i: (b, jnp.maximum(i - 1, 0), hp))
    out = pl.pallas_call(
        _moba_prompt_kernel,
        grid=(batch, npair, nb + 1),
        in_specs=[q_spec, kv_spec, kv_spec],
        out_specs=o_spec,
        out_shape=jax.ShapeDtypeStruct((batch, t, W_A), BF16),
        scratch_shapes=[pltpu.VMEM((nb, 2, PV_ROWS, MOBA_BLOCK), BF16),
                        pltpu.VMEM((2, LANES, MOBA_BLOCK), BF16),
                        pltpu.VMEM((2, nb, MOBA_BLOCK, MOBA_BLOCK), F32),
                        pltpu.VMEM((2, nb, MOBA_BLOCK, MOBA_BLOCK), F32),
                        pltpu.VMEM((2, nb, MOBA_BLOCK), F32),
                        pltpu.VMEM((2, nb, MOBA_BLOCK), F32),
                        pltpu.VMEM((2, nb, MOBA_BLOCK), F32),
                        pltpu.VMEM((2, nb, MOBA_BLOCK), F32),
                        pltpu.VMEM((2, PV_ROWS, MOBA_BLOCK), F32)],
        compiler_params=_cparams(3),
        name="moba_prompt",
    )(q3, k4, v4)
    return out.reshape(batch * t, W_A)


def _lane_head(lane):
    return lax.shift_right_logical(lane, DH_A.bit_length() - 1)


def _moba_sample_kernel(pt_ref, q_ref, kn_ref, vn_ref, ck_ref, cv_ref, o_ref, ring_ref, sem_ref,
                        lg_ref, muse_ref, acc_ref, lsum_ref, qrows_ref, *, layer, cpp, n_pages):
    ts = q_ref.shape[0]
    nrow = H_A * ts
    page = ring_ref.shape[4]
    nslot = ring_ref.shape[0]
    nck = n_pages // cpp
    nchunks = 2 * nck
    ppb = MOBA_BLOCK // page
    nb = n_pages // ppb
    b = pl.program_id(0)
    lane = lax.broadcasted_iota(jnp.int32, (nrow, LANES), 1)

    def chunk_copies(bb, c):
        src, p0, slot = (ck_ref, c * cpp, c % nslot) if c < nck else (cv_ref, (c - nck) * cpp, c % nslot)
        return [pltpu.make_async_copy(src.at[layer, pt_ref[bb, p0 + tt]], ring_ref.at[slot, tt], sem_ref.at[slot])
                for tt in range(cpp)]

    def start_chunk(bb, c):
        for cp in chunk_copies(bb, c):
            cp.start()

    def wait_chunk(bb, c):
        for cp in chunk_copies(bb, c):
            cp.wait()

    def prefetch(c):
        ahead = c + nslot - 1
        if ahead < nchunks:
            start_chunk(b, ahead)
        else:
            @pl.when(b + 1 < pl.num_programs(0))
            def _():
                start_chunk(b + 1, ahead - nchunks)

    @pl.when(b == 0)
    def _():
        for c in range(nslot - 1):
            start_chunk(0, c)

    qt = jnp.concatenate([q_ref[...]] * H_A, axis=0)
    r = lax.broadcasted_iota(jnp.int32, qt.shape, 0)
    hl = _lane_head(lax.broadcasted_iota(jnp.int32, qt.shape, 1))
    qrows_ref[...] = jnp.where((r >= hl * ts) & (r < hl * ts + ts), qt, 0.0).astype(BF16)
    acc_ref[...] = jnp.zeros(acc_ref.shape, F32)
    lsum_ref[...] = jnp.zeros(lsum_ref.shape, F32)

    bsum = jnp.full((nrow, LANES), -jnp.inf, F32)
    bmax = jnp.full((nrow, LANES), NEG, F32)
    for c in range(nck):
        wait_chunk(b, c)
        prefetch(c)
        qrb = qrows_ref[...]
        for tt in range(cpp):
            ktp = ring_ref[c % nslot, tt].reshape(W_A, page).astype(BF16)
            lg = _dot(qrb, ktp)
            lg_ref[c * cpp + tt] = lg
            if tt % ppb == 0:
                s_acc, m_acc = lg, lg
            else:
                s_acc, m_acc = s_acc + lg, jnp.maximum(m_acc, lg)
            if tt % ppb == ppb - 1:
                here = lane == (c * cpp + tt) // ppb
                bsum = jnp.where(here, jnp.sum(s_acc, axis=1, keepdims=True), bsum)
                bmax = jnp.where(here, jnp.max(m_acc, axis=1, keepdims=True), bmax)

    sel = (_top3(bsum, 1) > 0.0) & (lane < nb)
    kn = jnp.concatenate([kn_ref[...], jnp.zeros((page - ts, W_A), F32)], axis=0).astype(BF16)
    lo = _dot_nt(qrows_ref[...], kn)
    kpos = lax.broadcasted_iota(jnp.int32, lo.shape, 1)
    qidx = lax.broadcasted_iota(jnp.int32, lo.shape, 0) & (ts - 1)
    lo = jnp.where((kpos < ts) & (kpos <= qidx), lo, NEG)
    lg_ref[n_pages] = lo
    m = jnp.maximum(jnp.max(lo, axis=1, keepdims=True),
                    jnp.max(jnp.where(sel, bmax, NEG), axis=1, keepdims=True))
    muse_ref[...] = jnp.where(sel | (lane == nb), m, BIG)

    def attend(g, j, vt_b):
        mj = jnp.sum(jnp.where(lane == j, muse_ref[...], 0.0), axis=1, keepdims=True)
        p = jnp.exp2((lg_ref[g] - mj).astype(BF16))
        lsum_ref[...] += p.astype(F32)
        acc_ref[...] += _dot_nt(p, vt_b)

    for c in range(nck, nchunks):
        wait_chunk(b, c)
        prefetch(c)
        for tt in range(cpp):
            g = (c - nck) * cpp + tt
            attend(g, g // ppb, ring_ref[c % nslot, tt].reshape(W_A, page).astype(BF16))

    vn = jnp.concatenate([vn_ref[...], jnp.zeros((page - ts, W_A), F32)], axis=0)
    attend(n_pages, nb, vn.T.astype(BF16))
    acc = acc_ref[...] / jnp.sum(lsum_ref[...], axis=1, keepdims=True)
    a3 = acc.reshape(H_A, ts, W_A)
    hrow = lax.broadcasted_iota(jnp.int32, a3.shape, 0)
    hl3 = _lane_head(lax.broadcasted_iota(jnp.int32, a3.shape, 2))
    o_ref[...] = jnp.sum(jnp.where(hrow == hl3, a3, 0.0), axis=0).astype(o_ref.dtype)


RING_SLOTS = 4
RING_CHUNK_PAGES = 16


def _moba_sample(qa, kn, vn, cache_kt, cache_vt, page_table, layer, ts):
    db, n_pages = page_table.shape
    page = cache_kt.shape[4]
    assert page == LANES and n_pages * page // MOBA_BLOCK < LANES
    cpp = min(RING_CHUNK_PAGES, n_pages)
    assert n_pages % cpp == 0 and cpp % (MOBA_BLOCK // page) == 0
    assert (2 * n_pages // cpp) % RING_SLOTS == 0
    nrow = H_A * ts
    rows = pl.BlockSpec((ts, W_A), lambda b, pt: (b, 0))
    hbm = pl.BlockSpec(memory_space=pl.ANY)
    grid_spec = pltpu.PrefetchScalarGridSpec(
        num_scalar_prefetch=1,
        grid=(db,),
        in_specs=[rows, rows, rows, hbm, hbm],
        out_specs=rows,
        scratch_shapes=[pltpu.VMEM((RING_SLOTS, cpp, H_A, DH_A, page), F32),
                        pltpu.SemaphoreType.DMA((RING_SLOTS,)),
                        pltpu.VMEM((n_pages + 1, nrow, page), F32),
                        pltpu.VMEM((nrow, LANES), F32),
                        pltpu.VMEM((nrow, W_A), F32),
                        pltpu.VMEM((nrow, page), F32),
                        pltpu.VMEM((nrow, W_A), BF16)],
    )
    return pl.pallas_call(
        functools.partial(_moba_sample_kernel, layer=layer, cpp=cpp, n_pages=n_pages),
        grid_spec=grid_spec,
        out_shape=jax.ShapeDtypeStruct((db * ts, W_A), F32),
        compiler_params=_cparams(1),
        name="moba_sample",
    )(page_table, qa, kn, vn, cache_kt, cache_vt)


def _ret_chunk(q, k, v, s, lg, c_true):
    n = q.shape[0]
    ii = lax.broadcasted_iota(jnp.int32, (n, n), 0)
    jj = lax.broadcasted_iota(jnp.int32, (n, n), 1)
    diff = (ii - jj).astype(F32)
    causal = diff >= 0
    dmat = jnp.where(causal, jnp.exp(jnp.where(causal, diff * lg, 0.0)), 0.0)
    ic = lax.broadcasted_iota(jnp.int32, (n, 1), 0).astype(F32)
    q_dec = jnp.exp((ic + 1.0) * lg)
    k_dec = jnp.exp((c_true - 1.0 - ic) * lg)
    c_dec = jnp.exp(jnp.full((1, 1), float(c_true), F32) * lg)
    att = _dot_nt(q, k) * dmat
    inner = _dot(att.astype(BF16), v)
    cross = _dot(q, s.astype(BF16)) * q_dec
    kd = (k.astype(F32) * k_dec).T.astype(BF16)
    s_new = s * c_dec + _dot(kd, v)
    return inner + cross, s_new


def _head_norm_gate(o, g):
    on = o * lax.rsqrt(jnp.mean(o * o, axis=-1, keepdims=True) + EPS)
    return g.astype(F32) * on.astype(F32)


def _ret_prompt_kernel(lg_ref, q_ref, k_ref, v_ref, g_ref, r_ref, s_ref):
    c = pl.program_id(1)

    @pl.when(c == 0)
    def _():
        s_ref[...] = jnp.zeros(s_ref.shape, F32)

    for cc in range(q_ref.shape[0] // RET_CHUNK):
        rows = slice(cc * RET_CHUNK, (cc + 1) * RET_CHUNK)
        for h in range(H_R):
            qk = slice(h * DK_R, (h + 1) * DK_R)
            vv = slice(h * DV_R, (h + 1) * DV_R)
            o, s_new = _ret_chunk(q_ref[rows, qk], k_ref[rows, qk], v_ref[rows, vv], s_ref[h], lg_ref[h],
                                  RET_CHUNK)
            s_ref[h] = s_new
            r_ref[rows, vv] = _head_norm_gate(o, g_ref[rows, vv]).astype(r_ref.dtype)


def _ret_prompt(log_g, qr, kr, vr, gr, batch, t):
    rows = 2 * RET_CHUNK if t % (2 * RET_CHUNK) == 0 else RET_CHUNK
    nc = t // rows
    qk = pl.BlockSpec((None, rows, W_RQK), lambda b, c, lg: (b, c, 0))
    vv = pl.BlockSpec((None, rows, W_RV), lambda b, c, lg: (b, c, 0))
    grid_spec = pltpu.PrefetchScalarGridSpec(
        num_scalar_prefetch=1,
        grid=(batch, nc),
        in_specs=[qk, qk, vv, vv],
        out_specs=[vv, pl.BlockSpec((None, H_R, DK_R, DV_R), lambda b, c, lg: (b, 0, 0, 0))],
    )
    r, s_fin = pl.pallas_call(
        _ret_prompt_kernel,
        grid_spec=grid_spec,
        out_shape=[jax.ShapeDtypeStruct((batch, t, W_RV), BF16),
                   jax.ShapeDtypeStruct((batch, H_R, DK_R, DV_R), F32)],
        compiler_params=_cparams(2),
        name="ret_prompt",
    )(log_g, qr.reshape(batch, t, W_RQK), kr.reshape(batch, t, W_RQK),
      vr.reshape(batch, t, W_RV), gr.reshape(batch, t, W_RV))
    return r.reshape(batch * t, W_RV), s_fin


def _ret_sample_kernel(lg_ref, q_ref, k_ref, v_ref, g_ref, s0_ref, r_ref, s_ref, *, ts, spb):
    pad = lambda a: jnp.concatenate(
        [a, jnp.zeros((RET_CHUNK - ts, a.shape[1]), a.dtype)], axis=0).astype(BF16)
    rows = []
    for sidx in range(spb):
        sl = slice(sidx * ts, (sidx + 1) * ts)
        heads = []
        for h in range(H_R):
            qk = slice(h * DK_R, (h + 1) * DK_R)
            vv = slice(h * DV_R, (h + 1) * DV_R)
            o, s_new = _ret_chunk(pad(q_ref[sl, qk]), pad(k_ref[sl, qk]), pad(v_ref[sl, vv]),
                                  s0_ref[sidx, h].astype(F32), lg_ref[h], ts)
            s_ref[sidx, h] = s_new.astype(s_ref.dtype)
            heads.append(_head_norm_gate(o[:ts], g_ref[sl, vv]))
        rows.append(jnp.concatenate(heads, axis=1))
    r_ref[...] = jnp.concatenate(rows, axis=0).astype(r_ref.dtype)


def _ret_sample(log_g, qr, kr, vr, gr, state, layer, db, ts):
    spb = 4 if db % 4 == 0 else 1
    assert (spb * ts) % BF16_ROWS == 0
    qk = pl.BlockSpec((spb * ts, W_RQK), lambda b, lg: (b, 0))
    vv = pl.BlockSpec((spb * ts, W_RV), lambda b, lg: (b, 0))
    grid_spec = pltpu.PrefetchScalarGridSpec(
        num_scalar_prefetch=1,
        grid=(db // spb,),
        in_specs=[qk, qk, vv, vv,
                  pl.BlockSpec((None, spb, H_R, DK_R, DV_R), lambda b, lg: (layer, b, 0, 0, 0))],
        out_specs=[vv, pl.BlockSpec((spb, H_R, DK_R, DV_R), lambda b, lg: (b, 0, 0, 0))],
    )
    return pl.pallas_call(
        functools.partial(_ret_sample_kernel, ts=ts, spb=spb),
        grid_spec=grid_spec,
        out_shape=[jax.ShapeDtypeStruct((db * ts, W_RV), BF16),
                   jax.ShapeDtypeStruct((db, H_R, DK_R, DV_R), state.dtype)],
        compiler_params=_cparams(1),
        name="ret_sample",
    )(log_g, qr, kr, vr, gr, state)


def _merge_kernel(x_ref, a_ref, r_ref, sga_ref, sgr_ref, wpa_ref, wpr_ref, wo_ref, g_ref, o_ref):
    merged = (sga_ref[...].astype(F32) * _dot(a_ref[...].astype(BF16), wpa_ref[...])
              + sgr_ref[...].astype(F32) * _dot(r_ref[...], wpr_ref[...]))
    y = _dot(merged.astype(BF16), wo_ref[...])
    o_ref[...] = x_ref[...] + _rms_scale(y, g_ref[...])


def _merge(x, a, r, sga, sgr, wpa, wpr, wo, g, tm, layer):
    n, d = x.shape
    rows = lambda width: pl.BlockSpec((tm, width), lambda i: (i, 0))
    return pl.pallas_call(
        _merge_kernel,
        grid=(n // tm,),
        in_specs=[rows(d), rows(W_A), rows(W_RV), rows(d), rows(d)]
        + [_layer_slab(p, layer) for p in (wpa, wpr, wo, g)],
        out_specs=rows(d),
        out_shape=jax.ShapeDtypeStruct((n, d), F32),
        compiler_params=_cparams(1),
        name="merge",
    )(x, a, r, sga, sgr, wpa, wpr, wo, g)


FF_CHUNK = 256


def _ffn_kernel(x_ref, gpre_ref, w1_ref, w2_ref, gpost_ref, o_ref):
    d_ff = w2_ref.shape[0]
    x = x_ref[...]
    h = _rms_scale(x, gpre_ref[...]).astype(BF16)
    y = jnp.zeros(x.shape, F32)
    for c0 in range(0, d_ff, FF_CHUNK):
        gt = _dot(h, w1_ref[:, c0:c0 + FF_CHUNK])
        up = _dot(h, w1_ref[:, d_ff + c0:d_ff + c0 + FF_CHUNK])
        act = (gt * jax.nn.sigmoid(gt) * up).astype(BF16)
        y = y + _dot(act, w2_ref[c0:c0 + FF_CHUNK, :])
    o_ref[...] = x + _rms_scale(y, gpost_ref[...])


def _ffn(x, gpre, w1, w2, gpost, tm, layer):
    n, d = x.shape
    assert w2.shape[1] % FF_CHUNK == 0
    rows = pl.BlockSpec((tm, d), lambda i: (i, 0))
    return pl.pallas_call(
        _ffn_kernel,
        grid=(n // tm,),
        in_specs=[rows] + [_layer_slab(p, layer) for p in (gpre, w1, w2, gpost)],
        out_specs=rows,
        out_shape=jax.ShapeDtypeStruct((n, d), F32),
        compiler_params=_cparams(1),
        name="ffn",
    )(x, gpre, w1, w2, gpost)


def _rope_tables(pos):
    p = pos.astype(F32)[:, None]
    n = pos.shape[0]
    fa = ROPE_THETA ** (-jnp.arange(0, ROT_DIM, 2, dtype=F32) / ROT_DIM)
    ang = p * fa[None, :]
    cos, sin = jnp.cos(ang), jnp.sin(ang)
    half = ROT_DIM // 2
    rest = DH_A - ROT_DIM
    z_h, z_r, o_r = jnp.zeros((n, half), F32), jnp.zeros((n, rest), F32), jnp.ones((n, rest), F32)
    two = lambda a: jnp.concatenate([a, a], axis=1)
    ca = two(jnp.concatenate([cos, cos, o_r], axis=1))
    s1 = two(jnp.concatenate([z_h, sin, z_r], axis=1))
    s2 = two(jnp.concatenate([-sin, z_h, z_r], axis=1))
    fr = 1.0 / (RET_THETA ** jnp.linspace(0.0, 1.0, DK_R // 2, dtype=F32))
    angr = p * fr[None, :]
    cr = two(jnp.cos(angr))
    sr = jnp.concatenate([-jnp.sin(angr), jnp.sin(angr)], axis=1)
    return ca, s1, s2, cr, sr


def _row_tile(n, cap):
    tm = min(n, cap)
    while n % tm:
        tm //= 2
    return tm


def kernel(x_prompt, x_sample, cache_k, cache_v, state_ret, page_table, g_mix_pre, w_in, w_proj_attn, w_proj_ret, w_out, g_mix_post, g_ffn_pre, w_ffn_in, w_ffn_out, g_ffn_post):
    batch, t, d = x_prompt.shape
    db, ts, _ = x_sample.shape
    depth = w_in.shape[0]
    page = cache_k.shape[2]
    n_pages = page_table.shape[1]
    past_len = n_pages * page
    assert cache_k.shape[3:] == (H_A, DH_A)
    assert t % MOBA_BLOCK == 0 and t % RET_CHUNK == 0 and past_len % MOBA_BLOCK == 0
    assert ts < RET_CHUNK and ts % 8 == 0 and ts & (ts - 1) == 0 and H_A * ts <= LANES
    assert MOBA_BLOCK % page == 0 and DH_A & (DH_A - 1) == 0

    tabs_p = _rope_tables(jnp.arange(t, dtype=jnp.int32))
    tabs_s = tuple(jnp.tile(tb, (db, 1)) for tb in _rope_tables(past_len + jnp.arange(ts, dtype=jnp.int32)))
    log_g = jnp.log(1.0 - 2.0 ** (-5.0 - jnp.arange(H_R, dtype=F32)))
    ckt = cache_k.transpose(0, 1, 3, 4, 2)
    cvt = cache_v.transpose(0, 1, 3, 4, 2)

    np_, ns_ = batch * t, db * ts
    tm_p = _row_tile(t, 512)
    tm_s = _row_tile(ns_, 512)
    xp = x_prompt.reshape(np_, d)
    xs = x_sample.reshape(ns_, d)
    w1, wpa, wpr, wo, wf1, wf2 = (w.astype(BF16) for w in (w_in, w_proj_attn, w_proj_ret, w_out,
                                                            w_ffn_in, w_ffn_out))
    g_pre, g_post, gf_pre, gf_post = (g.reshape(depth, 1, d) for g in (g_mix_pre, g_mix_post,
                                                                       g_ffn_pre, g_ffn_post))

    kt = jnp.zeros((depth, batch, W_A, t), F32)
    vt = jnp.zeros((depth, batch, W_A, t), F32)
    sp, kss, vss, sss = [], [], [], []
    for l in range(depth):
        qa, kt, vt, kb, vb, qr, kr, vr, gr, sga, sgr = _in_proj(
            xp, g_pre, w1, tabs_p, tm_p, t // tm_p, l, kv_stack=(kt, vt))
        a = _moba_prompt(qa, kb, vb, batch, t)
        r, s_fin = _ret_prompt(log_g, qr, kr, vr, gr, batch, t)
        xp = _merge(xp, a, r, sga, sgr, wpa, wpr, wo, g_post, tm_p, l)
        xp = _ffn(xp, gf_pre, wf1, wf2, gf_post, tm_p, l)
        sp.append(s_fin)

        qa, ka, va, kb, vb, qr, kr, vr, gr, sga, sgr = _in_proj(
            xs, g_pre, w1, tabs_s, tm_s, ns_ // tm_s, l)
        f = lambda v: v.astype(F32)
        a = _moba_sample(f(qa), f(kb), f(vb), ckt, cvt, page_table, l, ts)
        r, s_new = _ret_sample(log_g, f(qr), f(kr), f(vr), f(gr), state_ret, l, db, ts)
        xs = _merge(xs, a, r, sga, sgr, wpa, wpr, wo, g_post, tm_s, l)
        xs = _ffn(xs, gf_pre, wf1, wf2, gf_post, tm_s, l)
        kss.append(ka.reshape(db, ts, H_A, DH_A))
        vss.append(va.reshape(db, ts, H_A, DH_A))
        sss.append(s_new)

    heads_last = lambda a: a.reshape(depth, batch, H_A, DH_A, t).transpose(0, 1, 4, 2, 3)
    return (xp.reshape(batch, t, d), xs.reshape(db, ts, d), heads_last(kt), heads_last(vt), jnp.stack(sp),
            jnp.stack(kss), jnp.stack(vss), jnp.stack(sss))
```

```python
import functools
import math

import jax
import jax.numpy as jnp
from jax import lax
from jax.experimental import pallas as pl
from jax.experimental.pallas import tpu as pltpu

F32 = jnp.float32
BF16 = jnp.bfloat16

H_A = 8
DH_A = 64
MOBA_BLOCK = 256
MOBA_TOPK = 3
ROPE_THETA = 500000.0
ROT_DIM = DH_A // 4
H_R = 4
DK_R = 128
DV_R = 256
RET_CHUNK = 128
RET_THETA = 10000.0
EPS = 1e-6
NEG = -1e30
BIG = 1e30
W_A = H_A * DH_A
W_RQK = H_R * DK_R
W_RV = H_R * DV_R
Q_SCALE = DH_A ** -0.5 * math.log2(math.e)

LANES = 128
BF16_ROWS = 16
VMEM_LIMIT = 56 * 1024 * 1024
PV_ROWS = DH_A + BF16_ROWS
KV_GROUP = 4


def _cparams(n_grid):
    return pltpu.CompilerParams(dimension_semantics=("arbitrary",) * n_grid,
                                vmem_limit_bytes=VMEM_LIMIT)


def _layer_slab(stacked, layer):
    return pl.BlockSpec((None,) + stacked.shape[1:], lambda *_: (layer, 0, 0), pipeline_mode=pl.Buffered(1))


def _rms_scale(y, g):
    return y * lax.rsqrt(jnp.mean(y * y, axis=-1, keepdims=True) + EPS) * g


def _dot(a, b):
    return jnp.dot(a, b, preferred_element_type=F32)


def _dot_nt(a, b):
    return lax.dot_general(a, b, (((1,), (1,)), ((), ())), preferred_element_type=F32)


def _in_proj_kernel(x_ref, g_ref, w_ref, ca_ref, s1_ref, s2_ref, cr_ref, sr_ref, *rest, kv_transposed):
    if kv_transposed:
        rest = rest[2:]
    qa_ref, ka_ref, va_ref, kb_ref, vb_ref, qr_ref, kr_ref, vr_ref, gr_ref, sga_ref, sgr_ref = rest
    d_model = x_ref.shape[1]
    h = _rms_scale(x_ref[...], g_ref[...]).astype(BF16)

    def proj(c0, n):
        return _dot(h, w_ref[:, c0:c0 + n])

    ca, s1, s2 = ca_ref[...], s1_ref[...], s2_ref[...]
    cr, sr = cr_ref[...], sr_ref[...]

    def rope_a(z):
        outs = []
        for c in range(z.shape[1] // LANES):
            zc = z[:, c * LANES:(c + 1) * LANES]
            outs.append(zc * ca + pltpu.roll(zc, ROT_DIM // 2, 1) * s1
                        + pltpu.roll(zc, LANES - ROT_DIM // 2, 1) * s2)
        return jnp.concatenate(outs, axis=1)

    def rope_r(z):
        outs = []
        for c in range(z.shape[1] // LANES):
            zc = z[:, c * LANES:(c + 1) * LANES]
            outs.append(zc * cr + pltpu.roll(zc, DK_R // 2, 1) * sr)
        return jnp.concatenate(outs, axis=1)

    c0 = 0
    qa_ref[...] = (rope_a(proj(c0, W_A)) * Q_SCALE).astype(qa_ref.dtype)
    c0 += W_A
    ka = rope_a(proj(c0, W_A))
    ka_ref[...] = ka.T if kv_transposed else ka
    kb_ref[...] = ka.astype(BF16)
    c0 += W_A
    va = proj(c0, W_A)
    va_ref[...] = va.T if kv_transposed else va
    vb_ref[...] = va.astype(BF16)
    c0 += W_A
    qr_ref[...] = rope_r(proj(c0, W_RQK)).astype(qr_ref.dtype)
    c0 += W_RQK
    kr_ref[...] = (rope_r(proj(c0, W_RQK)) * (DK_R ** -0.5)).astype(kr_ref.dtype)
    c0 += W_RQK
    vr_ref[...] = proj(c0, W_RV).astype(vr_ref.dtype)
    c0 += W_RV
    gr = proj(c0, W_RV)
    gr_ref[...] = (gr * jax.nn.sigmoid(gr)).astype(gr_ref.dtype)
    c0 += W_RV
    sga_ref[...] = jax.nn.sigmoid(proj(c0, d_model)).astype(sga_ref.dtype)
    c0 += d_model
    sgr_ref[...] = jax.nn.sigmoid(proj(c0, d_model)).astype(sgr_ref.dtype)


def _in_proj(x, g, w, tabs, tm, n_tab_blocks, layer, kv_stack=None):
    n, d = x.shape
    rows = lambda width: pl.BlockSpec((tm, width), lambda i: (i, 0))
    tab = pl.BlockSpec((tm, LANES), lambda i: (i % n_tab_blocks, 0))
    widths = [W_A, W_A, W_A, W_A, W_A, W_RQK, W_RQK, W_RV, W_RV, d, d]
    dtypes = [BF16, F32, F32, BF16, BF16, BF16, BF16, BF16, BF16, BF16, BF16]
    in_specs = [rows(d), _layer_slab(g, layer), _layer_slab(w, layer)] + [tab] * 5
    out_specs = [rows(wd) for wd in widths]
    out_shape = [jax.ShapeDtypeStruct((n, wd), dt) for wd, dt in zip(widths, dtypes)]
    args = [x, g, w, *tabs]
    aliases = {}
    if kv_stack is not None:
        t_blocks = kv_stack[0].shape[3] // tm
        kv_spec = pl.BlockSpec((None, None, W_A, tm), lambda i: (layer, i // t_blocks, 0, i % t_blocks))
        for j, buf in enumerate(kv_stack):
            in_specs.append(pl.BlockSpec(memory_space=pl.ANY))
            aliases[len(args)] = 1 + j
            args.append(buf)
            out_specs[1 + j] = kv_spec
            out_shape[1 + j] = jax.ShapeDtypeStruct(buf.shape, buf.dtype)
    return pl.pallas_call(
        functools.partial(_in_proj_kernel, kv_transposed=kv_stack is not None),
        grid=(n // tm,),
        in_specs=in_specs,
        out_specs=out_specs,
        out_shape=out_shape,
        input_output_aliases=aliases,
        compiler_params=_cparams(1),
        name="in_proj",
    )(*args)


def _top3(s, axis):
    n = s.shape[axis]
    idx = lax.broadcasted_iota(jnp.int32, s.shape, axis).astype(F32)
    sel = jnp.zeros(s.shape, F32)
    for _ in range(MOBA_TOPK):
        m = jnp.max(s, axis=axis, keepdims=True)
        first = jnp.min(jnp.where(s == m, idx, float(n)), axis=axis, keepdims=True)
        pick = idx == first
        sel = jnp.where(pick, 1.0, sel)
        s = jnp.where(pick, -jnp.inf, s)
    return sel


def _moba_prompt_kernel(q_ref, k_ref, v_ref, o_ref, vt_ref, qt_ref, lg0_ref, lg1_ref, cmax_ref, csum_ref,
                        mu0_ref, mu1_ref, acc_ref, hold_ref):
    nb = k_ref.shape[0]
    nq = MOBA_BLOCK
    s = pl.program_id(2)

    @pl.when(s == 0)
    def _():
        cmax_ref[...] = jnp.zeros(cmax_ref.shape, F32)
        csum_ref[...] = jnp.zeros(csum_ref.shape, F32)
        pad = jnp.concatenate([jnp.ones((1, MOBA_BLOCK), F32),
                               jnp.zeros((PV_ROWS - DH_A - 1, MOBA_BLOCK), F32)], axis=0)
        for j in range(nb):
            vt = v_ref[j].astype(F32).T
            for hh in range(2):
                vt_ref[j, hh] = jnp.concatenate([vt[hh * DH_A:(hh + 1) * DH_A], pad], axis=0).astype(BF16)

    def step(i, q_rows, emit, lg_w, mu_w, lg_r, mu_r):
        @pl.when(i < nb)
        def _():
            qt = q_ref[q_rows, :].astype(F32).T
            drow = lax.broadcasted_iota(jnp.int32, qt.shape, 0)
            qt_ref[0] = jnp.where(drow < DH_A, qt, 0.0).astype(BF16)
            qt_ref[1] = jnp.where(drow >= DH_A, qt, 0.0).astype(BF16)

        for hh in range(2):
            acc_ref[hh] = jnp.zeros(acc_ref.shape[1:], F32)

        def logits_block(j):
            kblk = k_ref[j]
            for hh in range(2):
                lg = _dot(kblk, qt_ref[hh])
                lg_w[hh, j] = lg
                cmax_ref[hh, pl.ds(j, 1), :] = jnp.max(lg, axis=0, keepdims=True)
                csum_ref[hh, pl.ds(j, 1), :] = jnp.sum(lg, axis=0, keepdims=True)

        def probs_group(t):
            out = []
            for hh in range(2):
                ps, vs = [], []
                for g in range(KV_GROUP):
                    jj = t * KV_GROUP + g
                    jc = jnp.minimum(jj, i - 1)
                    x = lg_r[hh, jc] - mu_r[hh, pl.ds(jj, 1), :]
                    ps.append(jnp.exp2(x.astype(BF16)))
                    vs.append(vt_ref[jc, hh])
                out.append((jnp.concatenate(vs, axis=1), jnp.concatenate(ps, axis=0)))
            return out

        def values_group(pv):
            for hh in range(2):
                acc_ref[hh] += _dot(*pv[hh])

        def fused(t, carry):
            pv = probs_group(t)
            for g in range(KV_GROUP):
                logits_block(t * KV_GROUP + g)
            values_group(pv)
            return carry

        def values_only(t, carry):
            values_group(probs_group(t))
            return carry

        trips = (i + KV_GROUP - 1) // KV_GROUP
        lax.fori_loop(0, jnp.where(i < nb, trips, 0), fused, 0)
        lax.fori_loop(0, jnp.where(i == nb, trips, 0), values_only, 0)

        @pl.when(i < nb)
        def _():
            kpos = lax.broadcasted_iota(jnp.int32, (MOBA_BLOCK, nq), 0)
            qpos = lax.broadcasted_iota(jnp.int32, (MOBA_BLOCK, nq), 1)
            blk = lax.broadcasted_iota(jnp.int32, (nb, nq), 0)
            kown = k_ref[i]
            for hh in range(2):
                lg = jnp.where(kpos <= qpos, _dot(kown, qt_ref[hh]), NEG)
                lg_w[hh, i] = lg
                m = jnp.max(lg, axis=0, keepdims=True)
                sel = _top3(jnp.where(blk < i, csum_ref[hh], -jnp.inf), 0)
                sel = (sel > 0.0) & (blk < i)
                m = jnp.maximum(m, jnp.max(jnp.where(sel, cmax_ref[hh], NEG), axis=0, keepdims=True))
                mu_w[hh] = jnp.where(sel | (blk == i), m, BIG)

        @pl.when((i >= 1) & (i <= nb))
        def _():
            outs = []
            for hh in range(2):
                acc = acc_ref[hh]
                outs.append(acc[:DH_A] / acc[DH_A:DH_A + 1])
            emit(jnp.concatenate(outs, axis=0).T)

    def emit_odd(tile):
        o_ref[:MOBA_BLOCK, :] = hold_ref[...]
        o_ref[MOBA_BLOCK:, :] = tile.astype(o_ref.dtype)

    def emit_even(tile):
        hold_ref[...] = tile.astype(hold_ref.dtype)

    step(2 * s, slice(0, MOBA_BLOCK), emit_odd, lg0_ref, mu0_ref, lg1_ref, mu1_ref)
    step(2 * s + 1, slice(MOBA_BLOCK, 2 * MOBA_BLOCK), emit_even, lg1_ref, mu1_ref, lg0_ref, mu0_ref)


def _moba_prompt(qa, kb, vb, batch, t):
    nb = t // MOBA_BLOCK
    assert nb % KV_GROUP == 0 and nb % 2 == 0
    npair = W_A // LANES
    nstep = nb // 2
    q3 = qa.reshape(batch, t, W_A)
    k4 = kb.reshape(batch, nb, MOBA_BLOCK, W_A)
    v4 = vb.reshape(batch, nb, MOBA_BLOCK, W_A)
    kv_spec = pl.BlockSpec((None, nb, MOBA_BLOCK, LANES), lambda b, hp, s: (b, 0, 0, hp))
    q_spec = pl.BlockSpec((None, 2 * MOBA_BLOCK, LANES), lambda b, hp, s: (b, jnp.minimum(s, nstep - 1), hp))
    o_spec = pl.BlockSpec((None, 2 * MOBA_BLOCK, LANES), lambda b, hp, s: (b, jnp.maximum(s - 1, 0), hp))
    out = pl.pallas_call(
        _moba_prompt_kernel,
        grid=(batch, npair, nstep + 1),
        in_specs=[q_spec, kv_spec, kv_spec],
        out_specs=o_spec,
        out_shape=jax.ShapeDtypeStruct((batch, t, W_A), BF16),
        scratch_shapes=[pltpu.VMEM((nb, 2, PV_ROWS, MOBA_BLOCK), BF16),
                        pltpu.VMEM((2, LANES, MOBA_BLOCK), BF16),
                        pltpu.VMEM((2, nb, MOBA_BLOCK, MOBA_BLOCK), F32),
                        pltpu.VMEM((2, nb, MOBA_BLOCK, MOBA_BLOCK), F32),
                        pltpu.VMEM((2, nb, MOBA_BLOCK), F32),
                        pltpu.VMEM((2, nb, MOBA_BLOCK), F32),
                        pltpu.VMEM((2, nb, MOBA_BLOCK), F32),
                        pltpu.VMEM((2, nb, MOBA_BLOCK), F32),
                        pltpu.VMEM((2, PV_ROWS, MOBA_BLOCK), F32),
                        pltpu.VMEM((MOBA_BLOCK, LANES), BF16)],
        compiler_params=_cparams(3),
        name="moba_prompt",
    )(q3, k4, v4)
    return out.reshape(batch * t, W_A)


def _lane_head(lane):
    return lax.shift_right_logical(lane, DH_A.bit_length() - 1)


def _moba_sample_kernel(pt_ref, q_ref, kn_ref, vn_ref, ck_ref, cv_ref, o_ref, ring_ref, sem_ref,
                        lg_ref, muse_ref, acc_ref, lsum_ref, qrows_ref, *, layer, cpp, n_pages):
    ts = q_ref.shape[0]
    nrow = H_A * ts
    page = ring_ref.shape[4]
    nslot = ring_ref.shape[0]
    nck = n_pages // cpp
    nchunks = 2 * nck
    ppb = MOBA_BLOCK // page
    nb = n_pages // ppb
    b = pl.program_id(0)
    lane = lax.broadcasted_iota(jnp.int32, (nrow, LANES), 1)

    def chunk_copies(bb, c):
        src, p0, slot = (ck_ref, c * cpp, c % nslot) if c < nck else (cv_ref, (c - nck) * cpp, c % nslot)
        return [pltpu.make_async_copy(src.at[layer, pt_ref[bb, p0 + tt]], ring_ref.at[slot, tt], sem_ref.at[slot])
                for tt in range(cpp)]

    def start_chunk(bb, c):
        for cp in chunk_copies(bb, c):
            cp.start()

    def wait_chunk(bb, c):
        for cp in chunk_copies(bb, c):
            cp.wait()

    def prefetch(c):
        ahead = c + nslot - 1
        if ahead < nchunks:
            start_chunk(b, ahead)
        else:
            @pl.when(b + 1 < pl.num_programs(0))
            def _():
                start_chunk(b + 1, ahead - nchunks)

    @pl.when(b == 0)
    def _():
        for c in range(nslot - 1):
            start_chunk(0, c)

    qt = jnp.concatenate([q_ref[...]] * H_A, axis=0)
    r = lax.broadcasted_iota(jnp.int32, qt.shape, 0)
    hl = _lane_head(lax.broadcasted_iota(jnp.int32, qt.shape, 1))
    qrows_ref[...] = jnp.where((r >= hl * ts) & (r < hl * ts + ts), qt, 0.0).astype(BF16)
    acc_ref[...] = jnp.zeros(acc_ref.shape, F32)
    lsum_ref[...] = jnp.zeros(lsum_ref.shape, F32)

    bsum = jnp.full((nrow, LANES), -jnp.inf, F32)
    bmax = jnp.full((nrow, LANES), NEG, F32)
    for c in range(nck):
        wait_chunk(b, c)
        prefetch(c)
        qrb = qrows_ref[...]
        for tt in range(cpp):
            ktp = ring_ref[c % nslot, tt].reshape(W_A, page).astype(BF16)
            lg = _dot(qrb, ktp)
            lg_ref[c * cpp + tt] = lg
            if tt % ppb == 0:
                s_acc, m_acc = lg, lg
            else:
                s_acc, m_acc = s_acc + lg, jnp.maximum(m_acc, lg)
            if tt % ppb == ppb - 1:
                here = lane == (c * cpp + tt) // ppb
                bsum = jnp.where(here, jnp.sum(s_acc, axis=1, keepdims=True), bsum)
                bmax = jnp.where(here, jnp.max(m_acc, axis=1, keepdims=True), bmax)

    sel = (_top3(bsum, 1) > 0.0) & (lane < nb)
    kn = jnp.concatenate([kn_ref[...], jnp.zeros((page - ts, W_A), F32)], axis=0).astype(BF16)
    lo = _dot_nt(qrows_ref[...], kn)
    kpos = lax.broadcasted_iota(jnp.int32, lo.shape, 1)
    qidx = lax.broadcasted_iota(jnp.int32, lo.shape, 0) & (ts - 1)
    lo = jnp.where((kpos < ts) & (kpos <= qidx), lo, NEG)
    lg_ref[n_pages] = lo
    m = jnp.maximum(jnp.max(lo, axis=1, keepdims=True),
                    jnp.max(jnp.where(sel, bmax, NEG), axis=1, keepdims=True))
    muse_ref[...] = jnp.where(sel | (lane == nb), m, BIG)

    def attend(g, j, vt_b):
        mj = jnp.sum(jnp.where(lane == j, muse_ref[...], 0.0), axis=1, keepdims=True)
        p = jnp.exp2((lg_ref[g] - mj).astype(BF16))
        lsum_ref[...] += p.astype(F32)
        acc_ref[...] += _dot_nt(p, vt_b)

    for c in range(nck, nchunks):
        wait_chunk(b, c)
        prefetch(c)
        for tt in range(cpp):
            g = (c - nck) * cpp + tt
            attend(g, g // ppb, ring_ref[c % nslot, tt].reshape(W_A, page).astype(BF16))

    vn = jnp.concatenate([vn_ref[...], jnp.zeros((page - ts, W_A), F32)], axis=0)
    attend(n_pages, nb, vn.T.astype(BF16))
    acc = acc_ref[...] / jnp.sum(lsum_ref[...], axis=1, keepdims=True)
    a3 = acc.reshape(H_A, ts, W_A)
    hrow = lax.broadcasted_iota(jnp.int32, a3.shape, 0)
    hl3 = _lane_head(lax.broadcasted_iota(jnp.int32, a3.shape, 2))
    o_ref[...] = jnp.sum(jnp.where(hrow == hl3, a3, 0.0), axis=0).astype(o_ref.dtype)


RING_SLOTS = 4
RING_CHUNK_PAGES = 16


def _moba_sample(qa, kn, vn, cache_kt, cache_vt, page_table, layer, ts):
    db, n_pages = page_table.shape
    page = cache_kt.shape[4]
    assert page == LANES and n_pages * page // MOBA_BLOCK < LANES
    cpp = min(RING_CHUNK_PAGES, n_pages)
    assert n_pages % cpp == 0 and cpp % (MOBA_BLOCK // page) == 0
    assert (2 * n_pages // cpp) % RING_SLOTS == 0
    nrow = H_A * ts
    rows = pl.BlockSpec((ts, W_A), lambda b, pt: (b, 0))
    hbm = pl.BlockSpec(memory_space=pl.ANY)
    grid_spec = pltpu.PrefetchScalarGridSpec(
        num_scalar_prefetch=1,
        grid=(db,),
        in_specs=[rows, rows, rows, hbm, hbm],
        out_specs=rows,
        scratch_shapes=[pltpu.VMEM((RING_SLOTS, cpp, H_A, DH_A, page), F32),
                        pltpu.SemaphoreType.DMA((RING_SLOTS,)),
                        pltpu.VMEM((n_pages + 1, nrow, page), F32),
                        pltpu.VMEM((nrow, LANES), F32),
                        pltpu.VMEM((nrow, W_A), F32),
                        pltpu.VMEM((nrow, page), F32),
                        pltpu.VMEM((nrow, W_A), BF16)],
    )
    return pl.pallas_call(
        functools.partial(_moba_sample_kernel, layer=layer, cpp=cpp, n_pages=n_pages),
        grid_spec=grid_spec,
        out_shape=jax.ShapeDtypeStruct((db * ts, W_A), F32),
        compiler_params=_cparams(1),
        name="moba_sample",
    )(page_table, qa, kn, vn, cache_kt, cache_vt)


def _ret_chunk(q, k, v, s, lg, c_true):
    n = q.shape[0]
    ii = lax.broadcasted_iota(jnp.int32, (n, n), 0)
    jj = lax.broadcasted_iota(jnp.int32, (n, n), 1)
    diff = (ii - jj).astype(F32)
    causal = diff >= 0
    dmat = jnp.where(causal, jnp.exp(jnp.where(causal, diff * lg, 0.0)), 0.0)
    ic = lax.broadcasted_iota(jnp.int32, (n, 1), 0).astype(F32)
    q_dec = jnp.exp((ic + 1.0) * lg)
    k_dec = jnp.exp((c_true - 1.0 - ic) * lg)
    c_dec = jnp.exp(jnp.full((1, 1), float(c_true), F32) * lg)
    att = _dot_nt(q, k) * dmat
    inner = _dot(att.astype(BF16), v)
    cross = _dot(q, s.astype(BF16)) * q_dec
    kd = (k.astype(F32) * k_dec).T.astype(BF16)
    s_new = s * c_dec + _dot(kd, v)
    return inner + cross, s_new


def _head_norm_gate(o, g):
    on = o * lax.rsqrt(jnp.mean(o * o, axis=-1, keepdims=True) + EPS)
    return g.astype(F32) * on.astype(F32)


def _ret_prompt_kernel(lg_ref, q_ref, k_ref, v_ref, g_ref, r_ref, s_ref):
    c = pl.program_id(1)

    @pl.when(c == 0)
    def _():
        s_ref[...] = jnp.zeros(s_ref.shape, F32)

    for cc in range(q_ref.shape[0] // RET_CHUNK):
        rows = slice(cc * RET_CHUNK, (cc + 1) * RET_CHUNK)
        for h in range(H_R):
            qk = slice(h * DK_R, (h + 1) * DK_R)
            vv = slice(h * DV_R, (h + 1) * DV_R)
            o, s_new = _ret_chunk(q_ref[rows, qk], k_ref[rows, qk], v_ref[rows, vv], s_ref[h], lg_ref[h],
                                  RET_CHUNK)
            s_ref[h] = s_new
            r_ref[rows, vv] = _head_norm_gate(o, g_ref[rows, vv]).astype(r_ref.dtype)


def _ret_prompt(log_g, qr, kr, vr, gr, batch, t):
    rows = 2 * RET_CHUNK if t % (2 * RET_CHUNK) == 0 else RET_CHUNK
    nc = t // rows
    qk = pl.BlockSpec((None, rows, W_RQK), lambda b, c, lg: (b, c, 0))
    vv = pl.BlockSpec((None, rows, W_RV), lambda b, c, lg: (b, c, 0))
    grid_spec = pltpu.PrefetchScalarGridSpec(
        num_scalar_prefetch=1,
        grid=(batch, nc),
        in_specs=[qk, qk, vv, vv],
        out_specs=[vv, pl.BlockSpec((None, H_R, DK_R, DV_R), lambda b, c, lg: (b, 0, 0, 0))],
    )
    r, s_fin = pl.pallas_call(
        _ret_prompt_kernel,
        grid_spec=grid_spec,
        out_shape=[jax.ShapeDtypeStruct((batch, t, W_RV), BF16),
                   jax.ShapeDtypeStruct((batch, H_R, DK_R, DV_R), F32)],
        compiler_params=_cparams(2),
        name="ret_prompt",
    )(log_g, qr.reshape(batch, t, W_RQK), kr.reshape(batch, t, W_RQK),
      vr.reshape(batch, t, W_RV), gr.reshape(batch, t, W_RV))
    return r.reshape(batch * t, W_RV), s_fin


def _ret_sample_kernel(lg_ref, q_ref, k_ref, v_ref, g_ref, s0_ref, r_ref, s_ref, *, ts, spb):
    pad = lambda a: jnp.concatenate(
        [a, jnp.zeros((RET_CHUNK - ts, a.shape[1]), a.dtype)], axis=0).astype(BF16)
    rows = []
    for sidx in range(spb):
        sl = slice(sidx * ts, (sidx + 1) * ts)
        heads = []
        for h in range(H_R):
            qk = slice(h * DK_R, (h + 1) * DK_R)
            vv = slice(h * DV_R, (h + 1) * DV_R)
            o, s_new = _ret_chunk(pad(q_ref[sl, qk]), pad(k_ref[sl, qk]), pad(v_ref[sl, vv]),
                                  s0_ref[sidx, h].astype(F32), lg_ref[h], ts)
            s_ref[sidx, h] = s_new.astype(s_ref.dtype)
            heads.append(_head_norm_gate(o[:ts], g_ref[sl, vv]))
        rows.append(jnp.concatenate(heads, axis=1))
    r_ref[...] = jnp.concatenate(rows, axis=0).astype(r_ref.dtype)


def _ret_sample(log_g, qr, kr, vr, gr, state, layer, db, ts):
    spb = 4 if db % 4 == 0 else 1
    assert (spb * ts) % BF16_ROWS == 0
    qk = pl.BlockSpec((spb * ts, W_RQK), lambda b, lg: (b, 0))
    vv = pl.BlockSpec((spb * ts, W_RV), lambda b, lg: (b, 0))
    grid_spec = pltpu.PrefetchScalarGridSpec(
        num_scalar_prefetch=1,
        grid=(db // spb,),
        in_specs=[qk, qk, vv, vv,
                  pl.BlockSpec((None, spb, H_R, DK_R, DV_R), lambda b, lg: (layer, b, 0, 0, 0))],
        out_specs=[vv, pl.BlockSpec((spb, H_R, DK_R, DV_R), lambda b, lg: (b, 0, 0, 0))],
    )
    return pl.pallas_call(
        functools.partial(_ret_sample_kernel, ts=ts, spb=spb),
        grid_spec=grid_spec,
        out_shape=[jax.ShapeDtypeStruct((db * ts, W_RV), BF16),
                   jax.ShapeDtypeStruct((db, H_R, DK_R, DV_R), state.dtype)],
        compiler_params=_cparams(1),
        name="ret_sample",
    )(log_g, qr, kr, vr, gr, state)


def _merge_kernel(x_ref, a_ref, r_ref, sga_ref, sgr_ref, wpa_ref, wpr_ref, wo_ref, g_ref, o_ref):
    merged = (sga_ref[...].astype(F32) * _dot(a_ref[...].astype(BF16), wpa_ref[...])
              + sgr_ref[...].astype(F32) * _dot(r_ref[...], wpr_ref[...]))
    y = _dot(merged.astype(BF16), wo_ref[...])
    o_ref[...] = x_ref[...] + _rms_scale(y, g_ref[...])


def _merge(x, a, r, sga, sgr, wpa, wpr, wo, g, tm, layer):
    n, d = x.shape
    rows = lambda width: pl.BlockSpec((tm, width), lambda i: (i, 0))
    return pl.pallas_call(
        _merge_kernel,
        grid=(n // tm,),
        in_specs=[rows(d), rows(W_A), rows(W_RV), rows(d), rows(d)]
        + [_layer_slab(p, layer) for p in (wpa, wpr, wo, g)],
        out_specs=rows(d),
        out_shape=jax.ShapeDtypeStruct((n, d), F32),
        compiler_params=_cparams(1),
        name="merge",
    )(x, a, r, sga, sgr, wpa, wpr, wo, g)


FF_CHUNK = 256


def _ffn_kernel(x_ref, gpre_ref, w1_ref, w2_ref, gpost_ref, o_ref):
    d_ff = w2_ref.shape[0]
    x = x_ref[...]
    h = _rms_scale(x, gpre_ref[...]).astype(BF16)
    y = jnp.zeros(x.shape, F32)
    for c0 in range(0, d_ff, FF_CHUNK):
        gt = _dot(h, w1_ref[:, c0:c0 + FF_CHUNK])
        up = _dot(h, w1_ref[:, d_ff + c0:d_ff + c0 + FF_CHUNK])
        act = (gt * jax.nn.sigmoid(gt) * up).astype(BF16)
        y = y + _dot(act, w2_ref[c0:c0 + FF_CHUNK, :])
    o_ref[...] = x + _rms_scale(y, gpost_ref[...])


def _ffn(x, gpre, w1, w2, gpost, tm, layer):
    n, d = x.shape
    assert w2.shape[1] % FF_CHUNK == 0
    rows = pl.BlockSpec((tm, d), lambda i: (i, 0))
    return pl.pallas_call(
        _ffn_kernel,
        grid=(n // tm,),
        in_specs=[rows] + [_layer_slab(p, layer) for p in (gpre, w1, w2, gpost)],
        out_specs=rows,
        out_shape=jax.ShapeDtypeStruct((n, d), F32),
        compiler_params=_cparams(1),
        name="ffn",
    )(x, gpre, w1, w2, gpost)


def _rope_tables(pos):
    p = pos.astype(F32)[:, None]
    n = pos.shape[0]
    fa = ROPE_THETA ** (-jnp.arange(0, ROT_DIM, 2, dtype=F32) / ROT_DIM)
    ang = p * fa[None, :]
    cos, sin = jnp.cos(ang), jnp.sin(ang)
    half = ROT_DIM // 2
    rest = DH_A - ROT_DIM
    z_h, z_r, o_r = jnp.zeros((n, half), F32), jnp.zeros((n, rest), F32), jnp.ones((n, rest), F32)
    two = lambda a: jnp.concatenate([a, a], axis=1)
    ca = two(jnp.concatenate([cos, cos, o_r], axis=1))
    s1 = two(jnp.concatenate([z_h, sin, z_r], axis=1))
    s2 = two(jnp.concatenate([-sin, z_h, z_r], axis=1))
    fr = 1.0 / (RET_THETA ** jnp.linspace(0.0, 1.0, DK_R // 2, dtype=F32))
    angr = p * fr[None, :]
    cr = two(jnp.cos(angr))
    sr = jnp.concatenate([-jnp.sin(angr), jnp.sin(angr)], axis=1)
    return ca, s1, s2, cr, sr


def _row_tile(n, cap):
    tm = min(n, cap)
    while n % tm:
        tm //= 2
    return tm


def kernel(x_prompt, x_sample, cache_k, cache_v, state_ret, page_table, g_mix_pre, w_in, w_proj_attn, w_proj_ret, w_out, g_mix_post, g_ffn_pre, w_ffn_in, w_ffn_out, g_ffn_post):
    batch, t, d = x_prompt.shape
    db, ts, _ = x_sample.shape
    depth = w_in.shape[0]
    page = cache_k.shape[2]
    n_pages = page_table.shape[1]
    past_len = n_pages * page
    assert cache_k.shape[3:] == (H_A, DH_A)
    assert t % MOBA_BLOCK == 0 and t % RET_CHUNK == 0 and past_len % MOBA_BLOCK == 0
    assert ts < RET_CHUNK and ts % 8 == 0 and ts & (ts - 1) == 0 and H_A * ts <= LANES
    assert MOBA_BLOCK % page == 0 and DH_A & (DH_A - 1) == 0

    tabs_p = _rope_tables(jnp.arange(t, dtype=jnp.int32))
    tabs_s = tuple(jnp.tile(tb, (db, 1)) for tb in _rope_tables(past_len + jnp.arange(ts, dtype=jnp.int32)))
    log_g = jnp.log(1.0 - 2.0 ** (-5.0 - jnp.arange(H_R, dtype=F32)))
    ckt = cache_k.transpose(0, 1, 3, 4, 2)
    cvt = cache_v.transpose(0, 1, 3, 4, 2)

    np_, ns_ = batch * t, db * ts
    tm_p = _row_tile(t, 512)
    tm_s = _row_tile(ns_, 512)
    xp = x_prompt.reshape(np_, d)
    xs = x_sample.reshape(ns_, d)
    w1, wpa, wpr, wo, wf1, wf2 = (w.astype(BF16) for w in (w_in, w_proj_attn, w_proj_ret, w_out,
                                                            w_ffn_in, w_ffn_out))
    g_pre, g_post, gf_pre, gf_post = (g.reshape(depth, 1, d) for g in (g_mix_pre, g_mix_post,
                                                                       g_ffn_pre, g_ffn_post))

    kt = jnp.zeros((depth, batch, W_A, t), F32)
    vt = jnp.zeros((depth, batch, W_A, t), F32)
    sp, kss, vss, sss = [], [], [], []
    for l in range(depth):
        qa, kt, vt, kb, vb, qr, kr, vr, gr, sga, sgr = _in_proj(
            xp, g_pre, w1, tabs_p, tm_p, t // tm_p, l, kv_stack=(kt, vt))
        a = _moba_prompt(qa, kb, vb, batch, t)
        r, s_fin = _ret_prompt(log_g, qr, kr, vr, gr, batch, t)
        xp = _merge(xp, a, r, sga, sgr, wpa, wpr, wo, g_post, tm_p, l)
        xp = _ffn(xp, gf_pre, wf1, wf2, gf_post, tm_p, l)
        sp.append(s_fin)

        qa, ka, va, kb, vb, qr, kr, vr, gr, sga, sgr = _in_proj(
            xs, g_pre, w1, tabs_s, tm_s, ns_ // tm_s, l)
        f = lambda v: v.astype(F32)
        a = _moba_sample(f(qa), f(kb), f(vb), ckt, cvt, page_table, l, ts)
        r, s_new = _ret_sample(log_g, f(qr), f(kr), f(vr), f(gr), state_ret, l, db, ts)
        xs = _merge(xs, a, r, sga, sgr, wpa, wpr, wo, g_post, tm_s, l)
        xs = _ffn(xs, gf_pre, wf1, wf2, gf_post, tm_s, l)
        kss.append(ka.reshape(db, ts, H_A, DH_A))
        vss.append(va.reshape(db, ts, H_A, DH_A))
        sss.append(s_new)

    heads_last = lambda a: a.reshape(depth, batch, H_A, DH_A, t).transpose(0, 1, 4, 2, 3)
    return (xp.reshape(batch, t, d), xs.reshape(db, ts, d), heads_last(kt), heads_last(vt), jnp.stack(sp),
            jnp.stack(kss), jnp.stack(vss), jnp.stack(sss))
```

```python
import functools
import math

import jax
import jax.numpy as jnp
from jax import lax
from jax.experimental import pallas as pl
from jax.experimental.pallas import tpu as pltpu

F32 = jnp.float32
BF16 = jnp.bfloat16

H_A = 8
DH_A = 64
MOBA_BLOCK = 256
MOBA_TOPK = 3
ROPE_THETA = 500000.0
ROT_DIM = DH_A // 4
H_R = 4
DK_R = 128
DV_R = 256
RET_CHUNK = 128
RET_THETA = 10000.0
EPS = 1e-6
NEG = -1e30
BIG = 1e30
W_A = H_A * DH_A
W_RQK = H_R * DK_R
W_RV = H_R * DV_R
Q_SCALE = DH_A ** -0.5 * math.log2(math.e)

LANES = 128
BF16_ROWS = 16
VMEM_LIMIT = 56 * 1024 * 1024
PV_ROWS = DH_A + BF16_ROWS
KV_GROUP = 4


def _cparams(n_grid):
    return pltpu.CompilerParams(dimension_semantics=("arbitrary",) * n_grid,
                                vmem_limit_bytes=VMEM_LIMIT)


def _layer_slab(stacked, layer):
    return pl.BlockSpec((None,) + stacked.shape[1:], lambda *_: (layer, 0, 0), pipeline_mode=pl.Buffered(1))


def _rms_scale(y, g):
    return y * lax.rsqrt(jnp.mean(y * y, axis=-1, keepdims=True) + EPS) * g


def _dot(a, b):
    return jnp.dot(a, b, preferred_element_type=F32)


def _dot_nt(a, b):
    return lax.dot_general(a, b, (((1,), (1,)), ((), ())), preferred_element_type=F32)


def _in_proj_kernel(x_ref, g_ref, w_ref, ca_ref, s1_ref, s2_ref, cr_ref, sr_ref, *rest, kv_transposed):
    if kv_transposed:
        rest = rest[2:]
    qa_ref, ka_ref, va_ref, kb_ref, vb_ref, qr_ref, kr_ref, vr_ref, gr_ref, sga_ref, sgr_ref = rest
    d_model = x_ref.shape[1]
    h = _rms_scale(x_ref[...], g_ref[...]).astype(BF16)

    def proj(c0, n):
        return _dot(h, w_ref[:, c0:c0 + n])

    ca, s1, s2 = ca_ref[...], s1_ref[...], s2_ref[...]
    cr, sr = cr_ref[...], sr_ref[...]

    def rope_a(z):
        outs = []
        for c in range(z.shape[1] // LANES):
            zc = z[:, c * LANES:(c + 1) * LANES]
            outs.append(zc * ca + pltpu.roll(zc, ROT_DIM // 2, 1) * s1
                        + pltpu.roll(zc, LANES - ROT_DIM // 2, 1) * s2)
        return jnp.concatenate(outs, axis=1)

    def rope_r(z):
        outs = []
        for c in range(z.shape[1] // LANES):
            zc = z[:, c * LANES:(c + 1) * LANES]
            outs.append(zc * cr + pltpu.roll(zc, DK_R // 2, 1) * sr)
        return jnp.concatenate(outs, axis=1)

    c0 = 0
    qa_ref[...] = (rope_a(proj(c0, W_A)) * Q_SCALE).astype(qa_ref.dtype)
    c0 += W_A
    ka = rope_a(proj(c0, W_A))
    ka_ref[...] = ka.T if kv_transposed else ka
    kb_ref[...] = ka.astype(BF16)
    c0 += W_A
    va = proj(c0, W_A)
    va_ref[...] = va.T if kv_transposed else va
    vb_ref[...] = va.astype(BF16)
    c0 += W_A
    qr_ref[...] = rope_r(proj(c0, W_RQK)).astype(qr_ref.dtype)
    c0 += W_RQK
    kr_ref[...] = (rope_r(proj(c0, W_RQK)) * (DK_R ** -0.5)).astype(kr_ref.dtype)
    c0 += W_RQK
    vr_ref[...] = proj(c0, W_RV).astype(vr_ref.dtype)
    c0 += W_RV
    gr = proj(c0, W_RV)
    gr_ref[...] = (gr * jax.nn.sigmoid(gr)).astype(gr_ref.dtype)
    c0 += W_RV
    sga_ref[...] = jax.nn.sigmoid(proj(c0, d_model)).astype(sga_ref.dtype)
    c0 += d_model
    sgr_ref[...] = jax.nn.sigmoid(proj(c0, d_model)).astype(sgr_ref.dtype)


def _in_proj(x, g, w, tabs, tm, n_tab_blocks, layer, kv_stack=None):
    n, d = x.shape
    rows = lambda width: pl.BlockSpec((tm, width), lambda i: (i, 0))
    tab = pl.BlockSpec((tm, LANES), lambda i: (i % n_tab_blocks, 0))
    widths = [W_A, W_A, W_A, W_A, W_A, W_RQK, W_RQK, W_RV, W_RV, d, d]
    dtypes = [BF16, F32, F32, BF16, BF16, BF16, BF16, BF16, BF16, BF16, BF16]
    in_specs = [rows(d), _layer_slab(g, layer), _layer_slab(w, layer)] + [tab] * 5
    out_specs = [rows(wd) for wd in widths]
    out_shape = [jax.ShapeDtypeStruct((n, wd), dt) for wd, dt in zip(widths, dtypes)]
    args = [x, g, w, *tabs]
    aliases = {}
    if kv_stack is not None:
        t_blocks = kv_stack[0].shape[3] // tm
        kv_spec = pl.BlockSpec((None, None, W_A, tm), lambda i: (layer, i // t_blocks, 0, i % t_blocks))
        for j, buf in enumerate(kv_stack):
            in_specs.append(pl.BlockSpec(memory_space=pl.ANY))
            aliases[len(args)] = 1 + j
            args.append(buf)
            out_specs[1 + j] = kv_spec
            out_shape[1 + j] = jax.ShapeDtypeStruct(buf.shape, buf.dtype)
    return pl.pallas_call(
        functools.partial(_in_proj_kernel, kv_transposed=kv_stack is not None),
        grid=(n // tm,),
        in_specs=in_specs,
        out_specs=out_specs,
        out_shape=out_shape,
        input_output_aliases=aliases,
        compiler_params=_cparams(1),
        name="in_proj",
    )(*args)


def _top3(s, axis):
    n = s.shape[axis]
    idx = lax.broadcasted_iota(jnp.int32, s.shape, axis).astype(F32)
    sel = jnp.zeros(s.shape, F32)
    for _ in range(MOBA_TOPK):
        m = jnp.max(s, axis=axis, keepdims=True)
        first = jnp.min(jnp.where(s == m, idx, float(n)), axis=axis, keepdims=True)
        pick = idx == first
        sel = jnp.where(pick, 1.0, sel)
        s = jnp.where(pick, -jnp.inf, s)
    return sel


def _moba_prompt_kernel(q_ref, k_ref, v_ref, o_ref, vt_ref, qt_ref, lg0_ref, lg1_ref, cmax_ref, csum_ref,
                        mu0_ref, mu1_ref, acc_ref, hold_ref):
    nb = k_ref.shape[0]
    nq = MOBA_BLOCK
    s = pl.program_id(2)

    @pl.when(s == 0)
    def _():
        cmax_ref[...] = jnp.zeros(cmax_ref.shape, F32)
        csum_ref[...] = jnp.zeros(csum_ref.shape, F32)
        pad = jnp.concatenate([jnp.ones((1, MOBA_BLOCK), F32),
                               jnp.zeros((PV_ROWS - DH_A - 1, MOBA_BLOCK), F32)], axis=0)
        for j in range(nb):
            vt = v_ref[j].astype(F32).T
            for hh in range(2):
                vt_ref[j, hh] = jnp.concatenate([vt[hh * DH_A:(hh + 1) * DH_A], pad], axis=0).astype(BF16)

    def step(i, q_rows, emit, lg_w, mu_w, lg_r, mu_r):
        @pl.when(i < nb)
        def _():
            qt = q_ref[q_rows, :].astype(F32).T
            drow = lax.broadcasted_iota(jnp.int32, qt.shape, 0)
            qt_ref[0] = jnp.where(drow < DH_A, qt, 0.0).astype(BF16)
            qt_ref[1] = jnp.where(drow >= DH_A, qt, 0.0).astype(BF16)

        for hh in range(2):
            acc_ref[hh] = jnp.zeros(acc_ref.shape[1:], F32)

        def logits_block(j):
            kblk = k_ref[j]
            for hh in range(2):
                lg = _dot(kblk, qt_ref[hh])
                lg_w[hh, j] = lg
                cmax_ref[hh, pl.ds(j, 1), :] = jnp.max(lg, axis=0, keepdims=True)
                csum_ref[hh, pl.ds(j, 1), :] = jnp.sum(lg, axis=0, keepdims=True)

        def probs_group(t):
            out = []
            for hh in range(2):
                ps, vs = [], []
                for g in range(KV_GROUP):
                    jj = t * KV_GROUP + g
                    jc = jnp.minimum(jj, i - 1)
                    x = lg_r[hh, jc] - mu_r[hh, pl.ds(jj, 1), :]
                    ps.append(jnp.exp2(x.astype(BF16)))
                    vs.append(vt_ref[jc, hh])
                out.append((jnp.concatenate(vs, axis=1), jnp.concatenate(ps, axis=0)))
            return out

        def values_group(pv):
            for hh in range(2):
                acc_ref[hh] += _dot(*pv[hh])

        def fused(t, carry):
            pv = probs_group(t)
            for g in range(KV_GROUP):
                logits_block(t * KV_GROUP + g)
            values_group(pv)
            return carry

        def values_only(t, carry):
            values_group(probs_group(t))
            return carry

        trips = (i + KV_GROUP - 1) // KV_GROUP
        lax.fori_loop(0, jnp.where(i < nb, trips, 0), fused, 0)
        lax.fori_loop(0, jnp.where(i == nb, trips, 0), values_only, 0)

        @pl.when(i < nb)
        def _():
            kpos = lax.broadcasted_iota(jnp.int32, (MOBA_BLOCK, nq), 0)
            qpos = lax.broadcasted_iota(jnp.int32, (MOBA_BLOCK, nq), 1)
            blk = lax.broadcasted_iota(jnp.int32, (nb, nq), 0)
            kown = k_ref[i]
            for hh in range(2):
                lg = jnp.where(kpos <= qpos, _dot(kown, qt_ref[hh]), NEG)
                lg_w[hh, i] = lg
                m = jnp.max(lg, axis=0, keepdims=True)
                sel = _top3(jnp.where(blk < i, csum_ref[hh], -jnp.inf), 0)
                sel = (sel > 0.0) & (blk < i)
                m = jnp.maximum(m, jnp.max(jnp.where(sel, cmax_ref[hh], NEG), axis=0, keepdims=True))
                mu_w[hh] = jnp.where(sel | (blk == i), m, BIG)

        @pl.when((i >= 1) & (i <= nb))
        def _():
            outs = []
            for hh in range(2):
                acc = acc_ref[hh]
                outs.append(acc[:DH_A] / acc[DH_A:DH_A + 1])
            emit(jnp.concatenate(outs, axis=0).T)

    def emit_odd(tile):
        o_ref[:MOBA_BLOCK, :] = hold_ref[...]
        o_ref[MOBA_BLOCK:, :] = tile.astype(o_ref.dtype)

    def emit_even(tile):
        hold_ref[...] = tile.astype(hold_ref.dtype)

    step(2 * s, slice(0, MOBA_BLOCK), emit_odd, lg0_ref, mu0_ref, lg1_ref, mu1_ref)
    step(2 * s + 1, slice(MOBA_BLOCK, 2 * MOBA_BLOCK), emit_even, lg1_ref, mu1_ref, lg0_ref, mu0_ref)


def _moba_prompt(qa, kb, vb, batch, t):
    nb = t // MOBA_BLOCK
    assert nb % KV_GROUP == 0 and nb % 2 == 0
    npair = W_A // LANES
    nstep = nb // 2
    q3 = qa.reshape(batch, t, W_A)
    k4 = kb.reshape(batch, nb, MOBA_BLOCK, W_A)
    v4 = vb.reshape(batch, nb, MOBA_BLOCK, W_A)
    kv_spec = pl.BlockSpec((None, nb, MOBA_BLOCK, LANES), lambda b, hp, s: (b, 0, 0, hp))
    q_spec = pl.BlockSpec((None, 2 * MOBA_BLOCK, LANES), lambda b, hp, s: (b, jnp.minimum(s, nstep - 1), hp))
    o_spec = pl.BlockSpec((None, 2 * MOBA_BLOCK, LANES), lambda b, hp, s: (b, jnp.maximum(s - 1, 0), hp))
    out = pl.pallas_call(
        _moba_prompt_kernel,
        grid=(batch, npair, nstep + 1),
        in_specs=[q_spec, kv_spec, kv_spec],
        out_specs=o_spec,
        out_shape=jax.ShapeDtypeStruct((batch, t, W_A), BF16),
        scratch_shapes=[pltpu.VMEM((nb, 2, PV_ROWS, MOBA_BLOCK), BF16),
                        pltpu.VMEM((2, LANES, MOBA_BLOCK), BF16),
                        pltpu.VMEM((2, nb, MOBA_BLOCK, MOBA_BLOCK), F32),
                        pltpu.VMEM((2, nb, MOBA_BLOCK, MOBA_BLOCK), F32),
                        pltpu.VMEM((2, nb, MOBA_BLOCK), F32),
                        pltpu.VMEM((2, nb, MOBA_BLOCK), F32),
                        pltpu.VMEM((2, nb, MOBA_BLOCK), F32),
                        pltpu.VMEM((2, nb, MOBA_BLOCK), F32),
                        pltpu.VMEM((2, PV_ROWS, MOBA_BLOCK), F32),
                        pltpu.VMEM((MOBA_BLOCK, LANES), BF16)],
        compiler_params=_cparams(3),
        name="moba_prompt",
    )(q3, k4, v4)
    return out.reshape(batch * t, W_A)


def _lane_head(lane):
    return lax.shift_right_logical(lane, DH_A.bit_length() - 1)


def _moba_sample_kernel(pt_ref, q_ref, kn_ref, vn_ref, ck_ref, cv_ref, o_ref, ring_ref, sem_ref,
                        lg_ref, muse_ref, acc_ref, lsum_ref, qrows_ref, *, layer, cpp, n_pages):
    ts = q_ref.shape[0]
    nrow = H_A * ts
    page = ring_ref.shape[4]
    nslot = ring_ref.shape[0]
    nck = n_pages // cpp
    nchunks = 2 * nck
    ppb = MOBA_BLOCK // page
    nb = n_pages // ppb
    b = pl.program_id(0)
    lane = lax.broadcasted_iota(jnp.int32, (nrow, LANES), 1)

    def chunk_copies(bb, c):
        src, p0, slot = (ck_ref, c * cpp, c % nslot) if c < nck else (cv_ref, (c - nck) * cpp, c % nslot)
        return [pltpu.make_async_copy(src.at[layer, pt_ref[bb, p0 + tt]], ring_ref.at[slot, tt], sem_ref.at[slot])
                for tt in range(cpp)]

    def start_chunk(bb, c):
        for cp in chunk_copies(bb, c):
            cp.start()

    def wait_chunk(bb, c):
        for cp in chunk_copies(bb, c):
            cp.wait()

    def prefetch(c):
        ahead = c + nslot - 1
        if ahead < nchunks:
            start_chunk(b, ahead)
        else:
            @pl.when(b + 1 < pl.num_programs(0))
            def _():
                start_chunk(b + 1, ahead - nchunks)

    @pl.when(b == 0)
    def _():
        for c in range(nslot - 1):
            start_chunk(0, c)

    qt = jnp.concatenate([q_ref[...]] * H_A, axis=0)
    r = lax.broadcasted_iota(jnp.int32, qt.shape, 0)
    hl = _lane_head(lax.broadcasted_iota(jnp.int32, qt.shape, 1))
    qrows_ref[...] = jnp.where((r >= hl * ts) & (r < hl * ts + ts), qt, 0.0).astype(BF16)
    acc_ref[...] = jnp.zeros(acc_ref.shape, F32)
    lsum_ref[...] = jnp.zeros(lsum_ref.shape, F32)

    bsum = jnp.full((nrow, LANES), -jnp.inf, F32)
    bmax = jnp.full((nrow, LANES), NEG, F32)
    for c in range(nck):
        wait_chunk(b, c)
        prefetch(c)
        qrb = qrows_ref[...]
        for tt in range(cpp):
            ktp = ring_ref[c % nslot, tt].reshape(W_A, page).astype(BF16)
            lg = _dot(qrb, ktp)
            lg_ref[c * cpp + tt] = lg
            if tt % ppb == 0:
                s_acc, m_acc = lg, lg
            else:
                s_acc, m_acc = s_acc + lg, jnp.maximum(m_acc, lg)
            if tt % ppb == ppb - 1:
                here = lane == (c * cpp + tt) // ppb
                bsum = jnp.where(here, jnp.sum(s_acc, axis=1, keepdims=True), bsum)
                bmax = jnp.where(here, jnp.max(m_acc, axis=1, keepdims=True), bmax)

    sel = (_top3(bsum, 1) > 0.0) & (lane < nb)
    kn = jnp.concatenate([kn_ref[...], jnp.zeros((page - ts, W_A), F32)], axis=0).astype(BF16)
    lo = _dot_nt(qrows_ref[...], kn)
    kpos = lax.broadcasted_iota(jnp.int32, lo.shape, 1)
    qidx = lax.broadcasted_iota(jnp.int32, lo.shape, 0) & (ts - 1)
    lo = jnp.where((kpos < ts) & (kpos <= qidx), lo, NEG)
    lg_ref[n_pages] = lo
    m = jnp.maximum(jnp.max(lo, axis=1, keepdims=True),
                    jnp.max(jnp.where(sel, bmax, NEG), axis=1, keepdims=True))
    muse_ref[...] = jnp.where(sel | (lane == nb), m, BIG)

    def attend(g, j, vt_b):
        mj = jnp.sum(jnp.where(lane == j, muse_ref[...], 0.0), axis=1, keepdims=True)
        p = jnp.exp2((lg_ref[g] - mj).astype(BF16))
        lsum_ref[...] += p.astype(F32)
        acc_ref[...] += _dot_nt(p, vt_b)

    for c in range(nck, nchunks):
        wait_chunk(b, c)
        prefetch(c)
        for tt in range(cpp):
            g = (c - nck) * cpp + tt
            attend(g, g // ppb, ring_ref[c % nslot, tt].reshape(W_A, page).astype(BF16))

    vn = jnp.concatenate([vn_ref[...], jnp.zeros((page - ts, W_A), F32)], axis=0)
    attend(n_pages, nb, vn.T.astype(BF16))
    acc = acc_ref[...] / jnp.sum(lsum_ref[...], axis=1, keepdims=True)
    a3 = acc.reshape(H_A, ts, W_A)
    hrow = lax.broadcasted_iota(jnp.int32, a3.shape, 0)
    hl3 = _lane_head(lax.broadcasted_iota(jnp.int32, a3.shape, 2))
    o_ref[...] = jnp.sum(jnp.where(hrow == hl3, a3, 0.0), axis=0).astype(o_ref.dtype)


RING_SLOTS = 4
RING_CHUNK_PAGES = 16


def _moba_sample(qa, kn, vn, cache_kt, cache_vt, page_table, layer, ts):
    db, n_pages = page_table.shape
    page = cache_kt.shape[4]
    assert page == LANES and n_pages * page // MOBA_BLOCK < LANES
    cpp = min(RING_CHUNK_PAGES, n_pages)
    assert n_pages % cpp == 0 and cpp % (MOBA_BLOCK // page) == 0
    assert (2 * n_pages // cpp) % RING_SLOTS == 0
    nrow = H_A * ts
    rows = pl.BlockSpec((ts, W_A), lambda b, pt: (b, 0))
    hbm = pl.BlockSpec(memory_space=pl.ANY)
    grid_spec = pltpu.PrefetchScalarGridSpec(
        num_scalar_prefetch=1,
        grid=(db,),
        in_specs=[rows, rows, rows, hbm, hbm],
        out_specs=rows,
        scratch_shapes=[pltpu.VMEM((RING_SLOTS, cpp, H_A, DH_A, page), F32),
                        pltpu.SemaphoreType.DMA((RING_SLOTS,)),
                        pltpu.VMEM((n_pages + 1, nrow, page), F32),
                        pltpu.VMEM((nrow, LANES), F32),
                        pltpu.VMEM((nrow, W_A), F32),
                        pltpu.VMEM((nrow, page), F32),
                        pltpu.VMEM((nrow, W_A), BF16)],
    )
    return pl.pallas_call(
        functools.partial(_moba_sample_kernel, layer=layer, cpp=cpp, n_pages=n_pages),
        grid_spec=grid_spec,
        out_shape=jax.ShapeDtypeStruct((db * ts, W_A), F32),
        compiler_params=_cparams(1),
        name="moba_sample",
    )(page_table, qa, kn, vn, cache_kt, cache_vt)


def _ret_chunk(q, k, v, s, lg, c_true):
    n = q.shape[0]
    ii = lax.broadcasted_iota(jnp.int32, (n, n), 0)
    jj = lax.broadcasted_iota(jnp.int32, (n, n), 1)
    diff = (ii - jj).astype(F32)
    causal = diff >= 0
    dmat = jnp.where(causal, jnp.exp(jnp.where(causal, diff * lg, 0.0)), 0.0)
    ic = lax.broadcasted_iota(jnp.int32, (n, 1), 0).astype(F32)
    q_dec = jnp.exp((ic + 1.0) * lg)
    k_dec = jnp.exp((c_true - 1.0 - ic) * lg)
    c_dec = jnp.exp(jnp.full((1, 1), float(c_true), F32) * lg)
    att = _dot_nt(q, k) * dmat
    inner = _dot(att.astype(BF16), v)
    cross = _dot(q, s.astype(BF16)) * q_dec
    kd = (k.astype(F32) * k_dec).T.astype(BF16)
    s_new = s * c_dec + _dot(kd, v)
    return inner + cross, s_new


def _head_norm_gate(o, g):
    on = o * lax.rsqrt(jnp.mean(o * o, axis=-1, keepdims=True) + EPS)
    return g.astype(F32) * on.astype(F32)


def _ret_prompt_kernel(lg_ref, q_ref, k_ref, v_ref, g_ref, r_ref, s_ref):
    c = pl.program_id(1)

    @pl.when(c == 0)
    def _():
        s_ref[...] = jnp.zeros(s_ref.shape, F32)

    for cc in range(q_ref.shape[0] // RET_CHUNK):
        rows = slice(cc * RET_CHUNK, (cc + 1) * RET_CHUNK)
        for h in range(H_R):
            qk = slice(h * DK_R, (h + 1) * DK_R)
            vv = slice(h * DV_R, (h + 1) * DV_R)
            o, s_new = _ret_chunk(q_ref[rows, qk], k_ref[rows, qk], v_ref[rows, vv], s_ref[h], lg_ref[h],
                                  RET_CHUNK)
            s_ref[h] = s_new
            r_ref[rows, vv] = _head_norm_gate(o, g_ref[rows, vv]).astype(r_ref.dtype)


def _ret_prompt(log_g, qr, kr, vr, gr, batch, t):
    rows = 2 * RET_CHUNK if t % (2 * RET_CHUNK) == 0 else RET_CHUNK
    nc = t // rows
    qk = pl.BlockSpec((None, rows, W_RQK), lambda b, c, lg: (b, c, 0))
    vv = pl.BlockSpec((None, rows, W_RV), lambda b, c, lg: (b, c, 0))
    grid_spec = pltpu.PrefetchScalarGridSpec(
        num_scalar_prefetch=1,
        grid=(batch, nc),
        in_specs=[qk, qk, vv, vv],
        out_specs=[vv, pl.BlockSpec((None, H_R, DK_R, DV_R), lambda b, c, lg: (b, 0, 0, 0))],
    )
    r, s_fin = pl.pallas_call(
        _ret_prompt_kernel,
        grid_spec=grid_spec,
        out_shape=[jax.ShapeDtypeStruct((batch, t, W_RV), BF16),
                   jax.ShapeDtypeStruct((batch, H_R, DK_R, DV_R), F32)],
        compiler_params=_cparams(2),
        name="ret_prompt",
    )(log_g, qr.reshape(batch, t, W_RQK), kr.reshape(batch, t, W_RQK),
      vr.reshape(batch, t, W_RV), gr.reshape(batch, t, W_RV))
    return r.reshape(batch * t, W_RV), s_fin


def _ret_sample_kernel(lg_ref, q_ref, k_ref, v_ref, g_ref, s0_ref, r_ref, s_ref, *, ts, spb):
    pad = lambda a: jnp.concatenate(
        [a, jnp.zeros((RET_CHUNK - ts, a.shape[1]), a.dtype)], axis=0).astype(BF16)
    rows = []
    for sidx in range(spb):
        sl = slice(sidx * ts, (sidx + 1) * ts)
        heads = []
        for h in range(H_R):
            qk = slice(h * DK_R, (h + 1) * DK_R)
            vv = slice(h * DV_R, (h + 1) * DV_R)
            o, s_new = _ret_chunk(pad(q_ref[sl, qk]), pad(k_ref[sl, qk]), pad(v_ref[sl, vv]),
                                  s0_ref[sidx, h].astype(F32), lg_ref[h], ts)
            s_ref[sidx, h] = s_new.astype(s_ref.dtype)
            heads.append(_head_norm_gate(o[:ts], g_ref[sl, vv]))
        rows.append(jnp.concatenate(heads, axis=1))
    r_ref[...] = jnp.concatenate(rows, axis=0).astype(r_ref.dtype)


def _ret_sample(log_g, qr, kr, vr, gr, state, layer, db, ts):
    spb = 4 if db % 4 == 0 else 1
    assert (spb * ts) % BF16_ROWS == 0
    qk = pl.BlockSpec((spb * ts, W_RQK), lambda b, lg: (b, 0))
    vv = pl.BlockSpec((spb * ts, W_RV), lambda b, lg: (b, 0))
    grid_spec = pltpu.PrefetchScalarGridSpec(
        num_scalar_prefetch=1,
        grid=(db // spb,),
        in_specs=[qk, qk, vv, vv,
                  pl.BlockSpec((None, spb, H_R, DK_R, DV_R), lambda b, lg: (layer, b, 0, 0, 0))],
        out_specs=[vv, pl.BlockSpec((spb, H_R, DK_R, DV_R), lambda b, lg: (b, 0, 0, 0))],
    )
    return pl.pallas_call(
        functools.partial(_ret_sample_kernel, ts=ts, spb=spb),
        grid_spec=grid_spec,
        out_shape=[jax.ShapeDtypeStruct((db * ts, W_RV), BF16),
                   jax.ShapeDtypeStruct((db, H_R, DK_R, DV_R), state.dtype)],
        compiler_params=_cparams(1),
        name="ret_sample",
    )(log_g, qr, kr, vr, gr, state)


def _merge_kernel(x_ref, a_ref, r_ref, sga_ref, sgr_ref, wpa_ref, wpr_ref, wo_ref, g_ref, o_ref):
    merged = (sga_ref[...].astype(F32) * _dot(a_ref[...].astype(BF16), wpa_ref[...])
              + sgr_ref[...].astype(F32) * _dot(r_ref[...], wpr_ref[...]))
    y = _dot(merged.astype(BF16), wo_ref[...])
    o_ref[...] = x_ref[...] + _rms_scale(y, g_ref[...])


def _merge(x, a, r, sga, sgr, wpa, wpr, wo, g, tm, layer):
    n, d = x.shape
    rows = lambda width: pl.BlockSpec((tm, width), lambda i: (i, 0))
    return pl.pallas_call(
        _merge_kernel,
        grid=(n // tm,),
        in_specs=[rows(d), rows(W_A), rows(W_RV), rows(d), rows(d)]
        + [_layer_slab(p, layer) for p in (wpa, wpr, wo, g)],
        out_specs=rows(d),
        out_shape=jax.ShapeDtypeStruct((n, d), F32),
        compiler_params=_cparams(1),
        name="merge",
    )(x, a, r, sga, sgr, wpa, wpr, wo, g)


FF_CHUNK = 256


def _ffn_kernel(x_ref, gpre_ref, w1_ref, w2_ref, gpost_ref, o_ref):
    d_ff = w2_ref.shape[0]
    x = x_ref[...]
    h = _rms_scale(x, gpre_ref[...]).astype(BF16)
    y = jnp.zeros(x.shape, F32)
    for c0 in range(0, d_ff, FF_CHUNK):
        gt = _dot(h, w1_ref[:, c0:c0 + FF_CHUNK])
        up = _dot(h, w1_ref[:, d_ff + c0:d_ff + c0 + FF_CHUNK])
        act = (gt * jax.nn.sigmoid(gt) * up).astype(BF16)
        y = y + _dot(act, w2_ref[c0:c0 + FF_CHUNK, :])
    o_ref[...] = x + _rms_scale(y, gpost_ref[...])


def _ffn(x, gpre, w1, w2, gpost, tm, layer):
    n, d = x.shape
    assert w2.shape[1] % FF_CHUNK == 0
    rows = pl.BlockSpec((tm, d), lambda i: (i, 0))
    return pl.pallas_call(
        _ffn_kernel,
        grid=(n // tm,),
        in_specs=[rows] + [_layer_slab(p, layer) for p in (gpre, w1, w2, gpost)],
        out_specs=rows,
        out_shape=jax.ShapeDtypeStruct((n, d), F32),
        compiler_params=_cparams(1),
        name="ffn",
    )(x, gpre, w1, w2, gpost)


def _rope_tables(pos):
    p = pos.astype(F32)[:, None]
    half = ROT_DIM // 2
    fa = ROPE_THETA ** (-jnp.arange(0, ROT_DIM, 2, dtype=F32) / ROT_DIM)
    fa_lane = jnp.tile(jnp.concatenate([fa, fa, jnp.zeros((DH_A - ROT_DIM,), F32)]), LANES // DH_A)
    dim = jnp.arange(LANES) % DH_A
    ang = p * fa_lane[None, :]
    cos, sin = jnp.cos(ang), jnp.sin(ang)
    ca = cos
    s1 = jnp.where((dim >= half)[None, :], sin, 0.0)
    s2 = jnp.where((dim < half)[None, :], -sin, 0.0)
    fr = 1.0 / (RET_THETA ** jnp.linspace(0.0, 1.0, DK_R // 2, dtype=F32))
    angr = p * jnp.tile(fr, 2)[None, :]
    cr = jnp.cos(angr)
    sr = jnp.where((jnp.arange(LANES) < DK_R // 2)[None, :], -jnp.sin(angr), jnp.sin(angr))
    return ca, s1, s2, cr, sr


def _row_tile(n, cap):
    tm = min(n, cap)
    while n % tm:
        tm //= 2
    return tm


def kernel(x_prompt, x_sample, cache_k, cache_v, state_ret, page_table, g_mix_pre, w_in, w_proj_attn, w_proj_ret, w_out, g_mix_post, g_ffn_pre, w_ffn_in, w_ffn_out, g_ffn_post):
    batch, t, d = x_prompt.shape
    db, ts, _ = x_sample.shape
    depth = w_in.shape[0]
    page = cache_k.shape[2]
    n_pages = page_table.shape[1]
    past_len = n_pages * page
    assert cache_k.shape[3:] == (H_A, DH_A)
    assert t % MOBA_BLOCK == 0 and t % RET_CHUNK == 0 and past_len % MOBA_BLOCK == 0
    assert ts < RET_CHUNK and ts % 8 == 0 and ts & (ts - 1) == 0 and H_A * ts <= LANES
    assert MOBA_BLOCK % page == 0 and DH_A & (DH_A - 1) == 0

    tabs_p = _rope_tables(jnp.arange(t, dtype=jnp.int32))
    tabs_s = tuple(jnp.tile(tb, (db, 1)) for tb in _rope_tables(past_len + jnp.arange(ts, dtype=jnp.int32)))
    log_g = jnp.log(1.0 - 2.0 ** (-5.0 - jnp.arange(H_R, dtype=F32)))
    ckt = cache_k.transpose(0, 1, 3, 4, 2)
    cvt = cache_v.transpose(0, 1, 3, 4, 2)

    np_, ns_ = batch * t, db * ts
    tm_p = _row_tile(t, 512)
    tm_s = _row_tile(ns_, 512)
    xp = x_prompt.reshape(np_, d)
    xs = x_sample.reshape(ns_, d)
    w1, wpa, wpr, wo, wf1, wf2 = (w.astype(BF16) for w in (w_in, w_proj_attn, w_proj_ret, w_out,
                                                            w_ffn_in, w_ffn_out))
    g_pre, g_post, gf_pre, gf_post = (g.reshape(depth, 1, d) for g in (g_mix_pre, g_mix_post,
                                                                       g_ffn_pre, g_ffn_post))

    kt = jnp.zeros((depth, batch, W_A, t), F32)
    vt = jnp.zeros((depth, batch, W_A, t), F32)
    sp, kss, vss, sss = [], [], [], []
    for l in range(depth):
        qa, kt, vt, kb, vb, qr, kr, vr, gr, sga, sgr = _in_proj(
            xp, g_pre, w1, tabs_p, tm_p, t // tm_p, l, kv_stack=(kt, vt))
        a = _moba_prompt(qa, kb, vb, batch, t)
        r, s_fin = _ret_prompt(log_g, qr, kr, vr, gr, batch, t)
        xp = _merge(xp, a, r, sga, sgr, wpa, wpr, wo, g_post, tm_p, l)
        xp = _ffn(xp, gf_pre, wf1, wf2, gf_post, tm_p, l)
        sp.append(s_fin)

        qa, ka, va, kb, vb, qr, kr, vr, gr, sga, sgr = _in_proj(
            xs, g_pre, w1, tabs_s, tm_s, ns_ // tm_s, l)
        f = lambda v: v.astype(F32)
        a = _moba_sample(f(qa), f(kb), f(vb), ckt, cvt, page_table, l, ts)
        r, s_new = _ret_sample(log_g, f(qr), f(kr), f(vr), f(gr), state_ret, l, db, ts)
        xs = _merge(xs, a, r, sga, sgr, wpa, wpr, wo, g_post, tm_s, l)
        xs = _ffn(xs, gf_pre, wf1, wf2, gf_post, tm_s, l)
        kss.append(ka.reshape(db, ts, H_A, DH_A))
        vss.append(va.reshape(db, ts, H_A, DH_A))
        sss.append(s_new)

    heads_last = lambda a: a.reshape(depth, batch, H_A, DH_A, t).transpose(0, 1, 4, 2, 3)
    return (xp.reshape(batch, t, d), xs.reshape(db, ts, d), heads_last(kt), heads_last(vt), jnp.stack(sp),
            jnp.stack(kss), jnp.stack(vss), jnp.stack(sss))
```
